```python
import jax, jax.numpy as jnp
from jax import lax
import numpy as np

D_MODEL = 1024
BATCH = 16
SEQ = 2048
DEPTH = 2

ATT_HEAD_DIM = 64
ATT_HEADS = 8
ATT_KV_HEADS = 2
ATT_GROUP = ATT_HEADS // ATT_KV_HEADS
ATT_WIDTH = ATT_HEADS * ATT_HEAD_DIM
ATT_KV_WIDTH = ATT_KV_HEADS * ATT_HEAD_DIM
WINDOW = 128
ATT_BLOCK = 128
ROPE_DIM = ATT_HEAD_DIM // 4
ROPE_THETA = 500000.0
MLSTM_HEADS = 4
MLSTM_HEAD_DIM = 128
MLSTM_WIDTH = MLSTM_HEADS * MLSTM_HEAD_DIM
MLSTM_CHUNK = 128
MLSTM_N_GATES = 4 * MLSTM_HEADS
CONV_K = 3
D_FF = ((8 * D_MODEL // 3 + 255) // 256) * 256
NORM_EPS = 1e-6
SPLIT_SIZES = (ATT_WIDTH, ATT_KV_WIDTH, ATT_KV_WIDTH, MLSTM_WIDTH, MLSTM_WIDTH, MLSTM_WIDTH, MLSTM_WIDTH, MLSTM_N_GATES, 2 * D_MODEL)
IN_WIDTH = sum(SPLIT_SIZES)

kernel_name = 'hybrid_bidir_swa_mlstm_macaron'


def rms_norm(x, g):
    xf = x.astype(jnp.float32)
    y = xf * lax.rsqrt(jnp.mean(xf * xf, axis=-1, keepdims=True) + NORM_EPS)
    return (y * g.astype(jnp.float32)).astype(x.dtype)


def swiglu(h, w_gate, w_up, w_down):
    return (jax.nn.silu(h @ w_gate) * (h @ w_up)) @ w_down


def partial_rope(t, positions):
    half = ROPE_DIM // 2
    inv_freq = jnp.power(jnp.float32(ROPE_THETA), -jnp.arange(half, dtype=jnp.float32) * (2.0 / ROPE_DIM))
    ang = positions.astype(jnp.float32)[:, :, None] * inv_freq
    cos = jnp.cos(ang)[:, :, None, :]
    sin = jnp.sin(ang)[:, :, None, :]
    tr = t[..., :ROPE_DIM].astype(jnp.float32)
    t1, t2 = tr[..., :half], tr[..., half:]
    rot = jnp.concatenate([t1 * cos - t2 * sin, t2 * cos + t1 * sin], axis=-1)
    return jnp.concatenate([rot.astype(t.dtype), t[..., ROPE_DIM:]], axis=-1)


def windowed_gqa_with_sink(q, k, v, sink):
    B, S, _, dh = q.shape
    wb = ATT_BLOCK
    nb = S // wb
    f32 = jnp.float32
    qb = q.astype(f32).reshape(B, nb, wb, ATT_KV_HEADS, ATT_GROUP, dh)
    pad = ((0, 0), (wb, wb), (0, 0), (0, 0))
    kp = jnp.pad(k.astype(f32), pad).reshape(B, nb + 2, wb, ATT_KV_HEADS, dh)
    vp = jnp.pad(v.astype(f32), pad).reshape(B, nb + 2, wb, ATT_KV_HEADS, dh)
    kb = jnp.concatenate([kp[:, :-2], kp[:, 1:-1], kp[:, 2:]], axis=2)
    vb = jnp.concatenate([vp[:, :-2], vp[:, 1:-1], vp[:, 2:]], axis=2)
    s = jnp.einsum('bnqhgd,bnkhd->bnhgqk', qb, kb) * (dh ** -0.5)
    qi = jnp.arange(nb)[:, None, None] * wb + jnp.arange(wb)[None, :, None]
    kj = jnp.arange(nb)[:, None, None] * wb - wb + jnp.arange(3 * wb)[None, None, :]
    valid = (jnp.abs(qi - kj) <= WINDOW) & (kj >= 0) & (kj < S)
    s = jnp.where(valid[None, :, None, None], s, -jnp.inf)
    sink_l = sink.astype(f32).reshape(1, 1, ATT_KV_HEADS, ATT_GROUP, 1, 1)
    m = jnp.maximum(jnp.max(s, axis=-1, keepdims=True), sink_l)
    p = jnp.exp(s - m)
    den = jnp.sum(p, axis=-1, keepdims=True) + jnp.exp(sink_l - m)
    o = jnp.einsum('bnhgqk,bnkhd->bnqhgd', p / den, vb)
    return o.reshape(B, S, ATT_HEADS * dh).astype(q.dtype)


def centred_depthwise_conv(u, w, b):
    S = u.shape[1]
    pad = CONV_K // 2
    up = jnp.pad(u, ((0, 0), (pad, pad), (0, 0)))
    out = up[:, 0:S] * w[0]
    for j in range(1, CONV_K):
        out = out + up[:, j:j + S] * w[j]
    return out + b


def mlstm_chunkwise(q, k, v, log_i, log_f):
    B, H, S, dk = q.shape
    dv = v.shape[-1]
    L = MLSTM_CHUNK
    nc = S // L
    q = q.reshape(B, H, nc, L, dk)
    k = k.reshape(B, H, nc, L, dk)
    v = v.reshape(B, H, nc, L, dv)
    li = log_i.reshape(B, H, nc, L)
    b = jnp.cumsum(log_f.reshape(B, H, nc, L), axis=-1)
    b_tot = b[..., -1]
    a = b_tot[..., None] - b + li
    a_max = jnp.max(a, axis=-1)
    w = jnp.exp(a - a_max[..., None])
    kw = k * w[..., None]
    C_loc = jnp.einsum('bhcsk,bhcsv->bhckv', kw, v)
    n_loc = jnp.sum(kw, axis=3)

    def step(carry, inp):
        C, n, m = carry
        C_l, n_l, am, bt = inp
        m_new = jnp.maximum(bt + m, am)
        s_p = jnp.exp(bt + m - m_new)
        s_l = jnp.exp(am - m_new)
        C_new = s_p[..., None, None] * C + s_l[..., None, None] * C_l
        n_new = s_p[..., None] * n + s_l[..., None] * n_l
        return (C_new, n_new, m_new), (C, n, m)

    init = (jnp.zeros((B, H, dk, dv), jnp.float32), jnp.zeros((B, H, dk), jnp.float32), jnp.zeros((B, H), jnp.float32))
    xs = (jnp.moveaxis(C_loc, 2, 0), jnp.moveaxis(n_loc, 2, 0), jnp.moveaxis(a_max, 2, 0), jnp.moveaxis(b_tot, 2, 0))
    _, (C_in, n_in, m_in) = lax.scan(step, init, xs)
    C_in = jnp.moveaxis(C_in, 0, 2)
    n_in = jnp.moveaxis(n_in, 0, 2)
    m_in = jnp.moveaxis(m_in, 0, 2)

    D = b[..., :, None] - b[..., None, :] + li[..., None, :]
    tri = jnp.tril(jnp.ones((L, L), dtype=bool))
    D = jnp.where(tri, D, -jnp.inf)
    inter = b + m_in[..., None]
    m_t = jnp.maximum(inter, jnp.max(D, axis=-1))
    P = jnp.exp(D - m_t[..., None])
    sc = jnp.einsum('bhctd,bhcsd->bhcts', q, k) * P
    scale_in = jnp.exp(inter - m_t)
    num = jnp.einsum('bhcts,bhcsv->bhctv', sc, v) + scale_in[..., None] * jnp.einsum('bhctk,bhckv->bhctv', q, C_in)
    den = jnp.sum(sc, axis=-1) + scale_in * jnp.einsum('bhctk,bhck->bhct', q, n_in)
    h = num / jnp.maximum(jnp.abs(den), jnp.exp(-m_t))[..., None]
    return h.reshape(B, H, S, dv)


def bidirectional_mlstm(q, k, v, o_pre, gate_pre, gate_bias, norm_g):
    B, S, _ = q.shape
    f32 = jnp.float32

    def heads(t):
        return t.astype(f32).reshape(B, S, MLSTM_HEADS, MLSTM_HEAD_DIM).transpose(0, 2, 1, 3)

    qh = heads(q)
    kh = heads(k) * (MLSTM_HEAD_DIM ** -0.5)
    vh = heads(v)
    g = (gate_pre.astype(f32) + gate_bias.astype(f32)).reshape(B, S, 4, MLSTM_HEADS).transpose(2, 0, 3, 1)
    h_fwd = mlstm_chunkwise(qh, kh, vh, g[0], jax.nn.log_sigmoid(g[1]))

    def flip(t):
        return jnp.flip(t, axis=2)

    h_bwd = flip(mlstm_chunkwise(flip(qh), flip(kh), flip(vh), flip(g[2]), jax.nn.log_sigmoid(flip(g[3]))))
    h = h_fwd + h_bwd
    mu = jnp.mean(h, axis=-1, keepdims=True)
    var = jnp.mean(jnp.square(h - mu), axis=-1, keepdims=True)
    h = (h - mu) * lax.rsqrt(var + NORM_EPS)
    h = h.transpose(0, 2, 1, 3).reshape(B, S, MLSTM_WIDTH) * norm_g.astype(f32)
    return (jax.nn.sigmoid(o_pre.astype(f32)) * h).astype(q.dtype)


def setup_inputs(seed: int = 0) -> dict:
    key = jax.random.key(seed)
    ks = jax.random.split(key, 26)
    f32 = jnp.float32
    L = DEPTH

    def dense(k, shape, fan_in):
        return jax.random.normal(k, shape, f32) * (fan_in ** -0.5)

    def gain(k, shape):
        return 1.0 + 0.02 * jax.random.normal(k, shape, f32)

    x = jax.random.normal(ks[0], (BATCH, SEQ, D_MODEL), f32)
    positions = jnp.arange(SEQ, dtype=jnp.int32)[None, :] + jax.random.randint(ks[1], (BATCH, 1), 0, 1024, dtype=jnp.int32)
    forget_base = jnp.linspace(3.0, 6.0, MLSTM_HEADS, dtype=f32)
    is_forget = jnp.array([0.0, 1.0, 0.0, 1.0], f32)
    mlstm_gate_bias = (0.1 * jax.random.normal(ks[8], (L, 4, MLSTM_HEADS), f32) + is_forget[None, :, None] * forget_base[None, None, :]).reshape(L, MLSTM_N_GATES)
    return {
        'x': x,
        'positions': positions,
        'ffn1_norm': gain(ks[2], (L, D_MODEL)),
        'ffn1_w_gate': dense(ks[3], (L, D_MODEL, D_FF), D_MODEL),
        'ffn1_w_up': dense(ks[4], (L, D_MODEL, D_FF), D_MODEL),
        'ffn1_w_down': dense(ks[5], (L, D_FF, D_MODEL), D_FF),
        'mix_norm': gain(ks[6], (L, D_MODEL)),
        'w_in': dense(ks[7], (L, D_MODEL, IN_WIDTH), D_MODEL),
        'mlstm_gate_bias': mlstm_gate_bias,
        'attn_q_norm': gain(ks[9], (L, ATT_HEAD_DIM)),
        'attn_k_norm': gain(ks[10], (L, ATT_HEAD_DIM)),
        'attn_sink': 0.5 * jax.random.normal(ks[11], (L, ATT_HEADS), f32),
        'mlstm_conv_w': dense(ks[12], (L, CONV_K, 2 * MLSTM_WIDTH), CONV_K),
        'mlstm_conv_b': 0.02 * jax.random.normal(ks[13], (L, 2 * MLSTM_WIDTH), f32),
        'mlstm_out_norm': gain(ks[14], (L, MLSTM_WIDTH)),
        'w_branch_attn': dense(ks[15], (L, ATT_WIDTH, D_MODEL), ATT_WIDTH),
        'w_branch_mlstm': dense(ks[16], (L, MLSTM_WIDTH, D_MODEL), MLSTM_WIDTH),
        'w_out': dense(ks[17], (L, D_MODEL, D_MODEL), D_MODEL),
        'ffn2_norm': gain(ks[18], (L, D_MODEL)),
        'ffn2_w_gate': dense(ks[19], (L, D_MODEL, D_FF), D_MODEL),
        'ffn2_w_up': dense(ks[20], (L, D_MODEL, D_FF), D_MODEL),
        'ffn2_w_down': dense(ks[21], (L, D_FF, D_MODEL), D_FF),
        'block_out_norm': gain(ks[22], (L, D_MODEL)),
    }


def reference(x, positions, ffn1_norm, ffn1_w_gate, ffn1_w_up, ffn1_w_down, mix_norm, w_in, mlstm_gate_bias, attn_q_norm, attn_k_norm, attn_sink, mlstm_conv_w, mlstm_conv_b, mlstm_out_norm, w_branch_attn, w_branch_mlstm, w_out, ffn2_norm, ffn2_w_gate, ffn2_w_up, ffn2_w_down, block_out_norm):
    B, S, _ = x.shape
    split_idx = np.cumsum(SPLIT_SIZES)[:-1].tolist()
    for l in range(DEPTH):
        x = x + 0.5 * swiglu(rms_norm(x, ffn1_norm[l]), ffn1_w_gate[l], ffn1_w_up[l], ffn1_w_down[l])

        h = rms_norm(x, mix_norm[l])
        proj = h @ w_in[l]
        qa, ka, va, qm, km, vm, om, gm, gmerge = jnp.split(proj, split_idx, axis=-1)

        qa = rms_norm(qa.reshape(B, S, ATT_HEADS, ATT_HEAD_DIM), attn_q_norm[l])
        ka = rms_norm(ka.reshape(B, S, ATT_KV_HEADS, ATT_HEAD_DIM), attn_k_norm[l])
        qa = partial_rope(qa, positions)
        ka = partial_rope(ka, positions)
        va = va.reshape(B, S, ATT_KV_HEADS, ATT_HEAD_DIM)
        y_a = windowed_gqa_with_sink(qa, ka, va, attn_sink[l])

        qk = jax.nn.silu(centred_depthwise_conv(jnp.concatenate([qm, km], axis=-1), mlstm_conv_w[l], mlstm_conv_b[l]))
        qm, km = jnp.split(qk, 2, axis=-1)
        y_m = bidirectional_mlstm(qm, km, vm, om, gm, mlstm_gate_bias[l], mlstm_out_norm[l])

        g_a, g_m = jnp.split(jax.nn.sigmoid(gmerge), 2, axis=-1)
        merged = g_a * (y_a @ w_branch_attn[l]) + g_m * (y_m @ w_branch_mlstm[l])
        x = x + merged @ w_out[l]

        x = x + 0.5 * swiglu(rms_norm(x, ffn2_norm[l]), ffn2_w_gate[l], ffn2_w_up[l], ffn2_w_down[l])
        x = rms_norm(x, block_out_norm[l])
    return x
```

```python
import functools

import numpy as np
import jax
import jax.numpy as jnp
from jax import lax
from jax.experimental import pallas as pl
from jax.experimental.pallas import tpu as pltpu

F32 = jnp.float32
BF16 = jnp.bfloat16

ATT_HEAD_DIM = 64
ATT_HEADS = 8
ATT_KV_HEADS = 2
ATT_WIDTH = ATT_HEADS * ATT_HEAD_DIM
ATT_KV_WIDTH = ATT_KV_HEADS * ATT_HEAD_DIM
WINDOW = 128
ATT_BLOCK = 128
ROPE_DIM = ATT_HEAD_DIM // 4
ROPE_THETA = 500000.0
MLSTM_HEADS = 4
MLSTM_HEAD_DIM = 128
MLSTM_WIDTH = MLSTM_HEADS * MLSTM_HEAD_DIM
MLSTM_CHUNK = 128
NORM_EPS = 1e-6
NEG_BIG = -1e30

V7X_LANES = 128
V7X_MXU_COLS = 256
V7X_VMEM_LIMIT_BYTES = 56 * 1024 * 1024

FFN_ROWS = 1024
PROJ_ROWS = 512
MERGE_ROWS = 1024
ROPE_ROWS = 2048


def _params(n_axes):
    return pltpu.CompilerParams(
        dimension_semantics=("parallel",) * n_axes,
        vmem_limit_bytes=V7X_VMEM_LIMIT_BYTES,
    )


def _resident(shape):
    nd = len(shape)
    return pl.BlockSpec(shape, lambda *_: (0,) * nd, pipeline_mode=pl.Buffered(1))


def _rows(tm, width):
    return pl.BlockSpec((tm, width), lambda i: (i, 0))


def _rms(x, gain):
    ms = jnp.mean(x * x, axis=-1, keepdims=True)
    return x * lax.rsqrt(ms + NORM_EPS) * gain


def _dot(a, b):
    return jnp.dot(a, b, preferred_element_type=F32)


def _dot_nt(a, b):
    return lax.dot_general(a, b, (((1,), (1,)), ((), ())), preferred_element_type=F32)


def _dot_tn(a, b):
    return lax.dot_general(a, b, (((0,), (0,)), ((), ())), preferred_element_type=F32)


def _log_sigmoid(x):
    return jnp.minimum(x, 0.0) - jnp.log1p(jnp.exp(-jnp.abs(x)))


def _rope_kernel(pos_ref, freq_ref, cos_ref, sin_ref):
    ang = pos_ref[...].astype(F32) * freq_ref[...]
    j = lax.broadcasted_iota(jnp.int32, ang.shape, 1) % ATT_HEAD_DIM
    half = ROPE_DIM // 2
    c = jnp.cos(ang)
    s = jnp.sin(ang)
    cos_ref[...] = jnp.where(j < ROPE_DIM, c, 1.0)
    sin_ref[...] = jnp.where(j < half, -s, jnp.where(j < ROPE_DIM, s, 0.0))


def _rope_tables(positions):
    t = positions.size
    half = ROPE_DIM // 2
    inv_freq = np.power(np.float32(ROPE_THETA),
                        -np.arange(half, dtype=np.float32) * np.float32(2.0 / ROPE_DIM)).astype(np.float32)
    lane = np.arange(V7X_LANES) % ATT_HEAD_DIM
    freq = np.where(lane < ROPE_DIM, inv_freq[lane % half], 0.0).astype(np.float32)[None, :]
    pos = positions.reshape(t, 1)
    return pl.pallas_call(
        _rope_kernel,
        out_shape=[jax.ShapeDtypeStruct((t, V7X_LANES), F32)] * 2,
        grid=(t // ROPE_ROWS,),
        in_specs=[_rows(ROPE_ROWS, 1), _resident((1, V7X_LANES))],
        out_specs=[_rows(ROPE_ROWS, V7X_LANES)] * 2,
        compiler_params=_params(1),
        name="rope_tables",
    )(pos, jnp.asarray(freq))


def _ffn_kernel(*refs, d_ff, final_norm):
    if final_norm:
        x_ref, g_ref, wg_ref, wu_ref, wd_ref, go_ref, o_ref, hn_ref, a_ref = refs
    else:
        x_ref, g_ref, wg_ref, wu_ref, wd_ref, o_ref, hn_ref, a_ref = refs
    hn_ref[...] = _rms(x_ref[...], g_ref[...]).astype(BF16)
    for c0 in range(0, d_ff, V7X_MXU_COLS):
        cols = slice(c0, c0 + V7X_MXU_COLS)
        hn = hn_ref[...]
        gate = _dot(hn, wg_ref[:, cols])
        up = _dot(hn, wu_ref[:, cols])
        a_ref[:, cols] = (gate * jax.nn.sigmoid(gate) * up).astype(BF16)
    out = x_ref[...] + 0.5 * _dot(a_ref[...], wd_ref[...])
    if final_norm:
        out = _rms(out, go_ref[...])
    o_ref[...] = out


def _ffn(x, gain, wg, wu, wd, final_gain=None):
    t, d = x.shape
    d_ff = wg.shape[1]
    assert d_ff % V7X_MXU_COLS == 0 and t % FFN_ROWS == 0
    final_norm = final_gain is not None
    args = [x, gain.reshape(1, d), wg, wu, wd]
    in_specs = [_rows(FFN_ROWS, d), _resident((1, d)), _resident((d, d_ff)),
                _resident((d, d_ff)), _resident((d_ff, d))]
    if final_norm:
        args.append(final_gain.reshape(1, d))
        in_specs.append(_resident((1, d)))
    return pl.pallas_call(
        functools.partial(_ffn_kernel, d_ff=d_ff, final_norm=final_norm),
        out_shape=jax.ShapeDtypeStruct((t, d), F32),
        grid=(t // FFN_ROWS,),
        in_specs=in_specs,
        out_specs=_rows(FFN_ROWS, d),
        scratch_shapes=[pltpu.VMEM((FFN_ROWS, d), BF16), pltpu.VMEM((FFN_ROWS, d_ff), BF16)],
        compiler_params=_params(1),
        name="ffn_final" if final_norm else "ffn",
    )(*args)


def _head_rms(x, ones_bd, gain):
    sq = x * x
    hi = sq.astype(BF16)
    lo = (sq - hi.astype(F32)).astype(BF16)
    ss = _dot(hi, ones_bd) + _dot(lo, ones_bd)
    return x * lax.rsqrt(ss * (1.0 / ATT_HEAD_DIM) + NORM_EPS) * gain


def _rope(x, cos, sin):
    n = x.shape[1]
    half = ROPE_DIM // 2
    j = lax.broadcasted_iota(jnp.int32, x.shape, 1) % ATT_HEAD_DIM
    partner = jnp.where(j < half, pltpu.roll(x, n - half, 1), pltpu.roll(x, half, 1))
    return x * cos + partner * sin


def _lo_hi(x):
    lo = lax.broadcasted_iota(jnp.int32, x.shape, 1) < ATT_HEAD_DIM
    xr = pltpu.roll(x, ATT_HEAD_DIM, 1)
    zero = jnp.zeros_like(x)
    return jnp.concatenate([jnp.where(lo, x, zero), jnp.where(lo, zero, xr),
                            jnp.where(lo, xr, zero), jnp.where(lo, zero, x)], axis=1)


def _proj_kernel(x_ref, g_ref, wqkv_ref, wqkm_ref, wvo_ref, wgc_ref, wgr_ref, bgc_ref, bgr_ref,
                 wmg_ref, qg_ref, kg_ref, cos_ref, sin_ref, ones_ref,
                 qa_ref, k4_ref, v4_ref, qm_ref, km_ref, vm_ref, og_ref, gi_ref, gf_ref, gr_ref,
                 ga_ref, gmm_ref, hn_ref):
    hn_ref[...] = _rms(x_ref[...], g_ref[...]).astype(BF16)
    hn = hn_ref[...]
    cos = cos_ref[...]
    sin = sin_ref[...]
    ones_bd = ones_ref[...]

    q = _dot(hn, wqkv_ref[:, :ATT_WIDTH])
    q = _head_rms(q, ones_bd, qg_ref[...])
    reps = ATT_WIDTH // V7X_LANES
    q = _rope(q, jnp.concatenate([cos] * reps, axis=1), jnp.concatenate([sin] * reps, axis=1))
    qa_ref[...] = (q * (ATT_HEAD_DIM ** -0.5)).astype(BF16)
    k = _dot(hn, wqkv_ref[:, ATT_WIDTH:ATT_WIDTH + ATT_KV_WIDTH])
    k = _head_rms(k, ones_bd[:ATT_KV_WIDTH, :ATT_KV_WIDTH], kg_ref[...])
    k4_ref[...] = _lo_hi(_rope(k, cos, sin)).astype(BF16)
    v = _dot(hn, wqkv_ref[:, ATT_WIDTH + ATT_KV_WIDTH:])
    v4_ref[...] = _lo_hi(v).astype(BF16)

    qm_ref[...] = _dot(hn, wqkm_ref[:, :MLSTM_WIDTH])
    km_ref[...] = _dot(hn, wqkm_ref[:, MLSTM_WIDTH:])
    vm_ref[...] = _dot(hn, wvo_ref[:, :MLSTM_WIDTH]).astype(BF16)
    og_ref[...] = jax.nn.sigmoid(_dot(hn, wvo_ref[:, MLSTM_WIDTH:])).astype(BF16)
    gc = _dot(hn, wgc_ref[...]) + bgc_ref[...]
    gi_ref[...] = gc[:, :V7X_LANES]
    gf_ref[...] = gc[:, V7X_LANES:]
    gr = _dot_nt(wgr_ref[...], hn) + bgr_ref[...]
    for c in range(gr_ref.shape[0]):
        gr_ref[c] = gr[:, c * MLSTM_CHUNK:(c + 1) * MLSTM_CHUNK]

    d = ga_ref.shape[1]
    ga_ref[...] = jax.nn.sigmoid(_dot(hn, wmg_ref[:, :d])).astype(BF16)
    gmm_ref[...] = jax.nn.sigmoid(_dot(hn, wmg_ref[:, d:])).astype(BF16)


def _proj(x, gain, w, cos_t, sin_t):
    t, d = x.shape
    tm = PROJ_ROWS
    nck = tm // MLSTM_CHUNK
    heads = np.arange(ATT_WIDTH) // ATT_HEAD_DIM
    ones_bd = jnp.asarray((heads[:, None] == heads[None, :]).astype(np.float32), dtype=BF16)
    out_shape = [
        jax.ShapeDtypeStruct((t, ATT_WIDTH), BF16),
        jax.ShapeDtypeStruct((t, 4 * V7X_LANES), BF16),
        jax.ShapeDtypeStruct((t, 4 * V7X_LANES), BF16),
        jax.ShapeDtypeStruct((t, MLSTM_WIDTH), F32),
        jax.ShapeDtypeStruct((t, MLSTM_WIDTH), F32),
        jax.ShapeDtypeStruct((t, MLSTM_WIDTH), BF16),
        jax.ShapeDtypeStruct((t, MLSTM_WIDTH), BF16),
        jax.ShapeDtypeStruct((t, V7X_LANES), F32),
        jax.ShapeDtypeStruct((t, V7X_LANES), F32),
        jax.ShapeDtypeStruct((t // MLSTM_CHUNK, 16, MLSTM_CHUNK), F32),
        jax.ShapeDtypeStruct((t, d), BF16),
        jax.ShapeDtypeStruct((t, d), BF16),
    ]
    out_specs = [
        _rows(tm, ATT_WIDTH), _rows(tm, 4 * V7X_LANES), _rows(tm, 4 * V7X_LANES),
        _rows(tm, MLSTM_WIDTH), _rows(tm, MLSTM_WIDTH), _rows(tm, MLSTM_WIDTH), _rows(tm, MLSTM_WIDTH),
        _rows(tm, V7X_LANES), _rows(tm, V7X_LANES),
        pl.BlockSpec((nck, 16, MLSTM_CHUNK), lambda i: (i, 0, 0)),
        _rows(tm, d), _rows(tm, d),
    ]
    args = [x, gain.reshape(1, d), w["qkv"], w["qkm"], w["vo"], w["gate_col"], w["gate_row"],
            w["bias_col"], w["bias_row"], w["merge"], w["q_gain"], w["k_gain"], cos_t, sin_t, ones_bd]
    in_specs = [_rows(tm, d)] + [_resident(a.shape) for a in args[1:12]] + [
        _rows(tm, V7X_LANES), _rows(tm, V7X_LANES), _resident(ones_bd.shape)]
    return pl.pallas_call(
        _proj_kernel,
        out_shape=out_shape,
        grid=(t // tm,),
        in_specs=in_specs,
        out_specs=out_specs,
        scratch_shapes=[pltpu.VMEM((tm, d), BF16)],
        compiler_params=_params(1),
        name="mixer_proj",
    )(*args)


def _attn_kernel(sink_ref, q_ref, k4_ref, v4_ref, o_ref, *, seq):
    blk = ATT_BLOCK
    band = 3 * blk
    nb = seq // blk
    group = ATT_HEADS // ATT_KV_HEADS

    def body(n, carry):
        r0 = pl.multiple_of(n * blk, blk)
        start = pl.multiple_of(jnp.clip(r0 - blk, 0, seq - band), blk)
        qb = q_ref[pl.ds(r0, blk), :]
        kb = k4_ref[pl.ds(start, band), :]
        vb = v4_ref[pl.ds(start, band), :]
        qi = r0 + lax.broadcasted_iota(jnp.int32, (blk, band), 0)
        kj = start + lax.broadcasted_iota(jnp.int32, (blk, band), 1)
        valid = jnp.abs(qi - kj) <= WINDOW
        lo = lax.broadcasted_iota(jnp.int32, (blk, V7X_LANES), 1) < ATT_HEAD_DIM
        outs = []
        for g in range(ATT_KV_HEADS):
            kv_lo = slice(2 * g * V7X_LANES, (2 * g + 1) * V7X_LANES)
            kv_hi = slice((2 * g + 1) * V7X_LANES, (2 * g + 2) * V7X_LANES)
            for p in range(group // 2):
                pair = g * (group // 2) + p
                qp = qb[:, pair * V7X_LANES:(pair + 1) * V7X_LANES]
                probs, invs = [], []
                for half, cols in enumerate((kv_lo, kv_hi)):
                    sink = sink_ref[2 * pair + half]
                    s = jnp.where(valid, _dot_nt(qp, kb[:, cols]), NEG_BIG)
                    m = jnp.maximum(jnp.max(s, axis=-1, keepdims=True), sink)
                    e = jnp.exp(s - m)
                    den = jnp.sum(e, axis=-1, keepdims=True) + jnp.exp(sink - m)
                    probs.append(e.astype(BF16))
                    invs.append(1.0 / den)
                o = _dot(probs[0], vb[:, kv_lo]) + _dot(probs[1], vb[:, kv_hi])
                outs.append(o * jnp.where(lo, invs[0], invs[1]))
        o_ref[pl.ds(r0, blk), :] = jnp.concatenate(outs, axis=1).astype(BF16)
        return carry

    lax.fori_loop(0, nb, body, 0)


def _attention(q, k4, v4, sink, batch, seq):
    t = q.shape[0]
    by_batch = lambda w: pl.BlockSpec((seq, w), lambda b: (b, 0))
    return pl.pallas_call(
        functools.partial(_attn_kernel, seq=seq),
        out_shape=jax.ShapeDtypeStruct((t, ATT_WIDTH), BF16),
        grid=(batch,),
        in_specs=[pl.BlockSpec(memory_space=pltpu.SMEM), by_batch(ATT_WIDTH),
                  by_batch(4 * V7X_LANES), by_batch(4 * V7X_LANES)],
        out_specs=by_batch(ATT_WIDTH),
        compiler_params=_params(1),
        name="window_attn",
    )(sink, q, k4, v4)


def _scan_rows(x, reverse):
    n = x.shape[0]
    idx = lax.broadcasted_iota(jnp.int32, x.shape, 0)
    d = 1
    while d < n:
        if reverse:
            x = x + jnp.where(idx < n - d, pltpu.roll(x, n - d, 0), 0.0)
        else:
            x = x + jnp.where(idx >= d, pltpu.roll(x, d, 0), 0.0)
        d *= 2
    return x


def _scan_lanes(x, reverse):
    n = x.shape[1]
    idx = lax.broadcasted_iota(jnp.int32, x.shape, 1)
    d = 1
    while d < n:
        if reverse:
            x = x + jnp.where(idx < n - d, pltpu.roll(x, n - d, 1), 0.0)
        else:
            x = x + jnp.where(idx >= d, pltpu.roll(x, d, 1), 0.0)
        d *= 2
    return x


def _mlstm_kernel(qraw_ref, kraw_ref, v_ref, og_ref, gi_ref, gf_ref, gr_ref, cw_ref, cb_ref, ng_ref,
                  y_ref,
                  qs_ref, ks_ref, b_ref, w_ref, r_ref, am_ref, bt_ref,
                  cst_ref, nst_ref, mst_ref, cin_ref, nin_ref, minf_ref, minb_ref, *, seq):
    L = MLSTM_CHUNK
    H = MLSTM_HEADS
    dh = MLSTM_HEAD_DIM
    nc = seq // L
    width = MLSTM_WIDTH

    def head(h):
        return slice(h * dh, (h + 1) * dh)

    def prep(c, carry):
        r0 = pl.multiple_of(c * L, L)
        rows = pl.ds(r0, L)
        rid = lax.broadcasted_iota(jnp.int32, (L, width), 0)
        has_prev = (c > 0).astype(F32)
        has_next = (c < nc - 1).astype(F32)
        prev_at = pl.multiple_of(jnp.maximum(r0 - 8, 0), 8)
        next_at = pl.multiple_of(jnp.minimum(r0 + L, seq - 8), 8)
        for idx, (raw_ref, dst_ref) in enumerate(((qraw_ref, qs_ref), (kraw_ref, ks_ref))):
            cols = slice(idx * width, (idx + 1) * width)
            cur = raw_ref[rows, :]
            prev_row = raw_ref[pl.ds(prev_at, 8), :][7:8, :] * has_prev
            next_row = raw_ref[pl.ds(next_at, 8), :][0:1, :] * has_next
            before = jnp.where(rid == 0, prev_row, pltpu.roll(cur, 1, 0))
            after = jnp.where(rid == L - 1, next_row, pltpu.roll(cur, L - 1, 0))
            u = (before * cw_ref[0:1, cols] + cur * cw_ref[1:2, cols] + after * cw_ref[2:3, cols]
                 + cb_ref[:, cols])
            u = u * jax.nn.sigmoid(u)
            if idx == 1:
                u = u * (dh ** -0.5)
            dst_ref[rows, :] = u.astype(BF16)

        lane = lax.broadcasted_iota(jnp.int32, (L, V7X_LANES), 1)
        gi = gi_ref[rows, :]
        lf = _log_sigmoid(gf_ref[rows, :])
        bc = jnp.where(lane < H, _scan_rows(lf, False), _scan_rows(lf, True))
        bt = jnp.where(lane[0:1] < H, bc[L - 1:L, :], bc[0:1, :])
        a = bt - bc + gi
        am = jnp.max(a, axis=0, keepdims=True)
        b_ref[rows, :] = bc
        w_ref[rows, :] = jnp.exp(a - am)
        am_ref[c] = am
        bt_ref[c] = bt

        g_rows = gr_ref[c]
        lfr = _log_sigmoid(g_rows[8:16, :])
        sub = lax.broadcasted_iota(jnp.int32, (8, L), 0)
        br = jnp.where(sub < H, _scan_lanes(lfr, False), _scan_lanes(lfr, True))
        r_ref[c] = g_rows[0:8, :] - br
        return carry

    lax.fori_loop(0, nc, prep, 0)

    cst_ref[...] = jnp.zeros_like(cst_ref)
    nst_ref[...] = jnp.zeros_like(nst_ref)
    mst_ref[...] = jnp.zeros_like(mst_ref)

    def scan(i, carry):
        c_f = i
        c_b = nc - 1 - i
        lane1 = lax.broadcasted_iota(jnp.int32, (1, V7X_LANES), 1)
        fwd = lane1 < H
        bt = jnp.where(fwd, bt_ref[c_f], bt_ref[c_b])
        am = jnp.where(fwd, am_ref[c_f], am_ref[c_b])
        m_old = mst_ref[...]
        m_new = jnp.maximum(bt + m_old, am)
        s_prev = jnp.exp(bt + m_old - m_new)
        s_loc = jnp.exp(am - m_new)
        minf_ref[c_f] = m_old
        minb_ref[c_b] = m_old
        mst_ref[...] = m_new
        for d, c in enumerate((c_f, c_b)):
            rows = pl.ds(pl.multiple_of(c * L, L), L)
            wc = w_ref[rows, :]
            for h in range(H):
                j = d * H + h
                kw = ks_ref[rows, head(h)].astype(F32) * wc[:, j:j + 1]
                c_loc = _dot_tn(kw.astype(BF16), v_ref[rows, head(h)])
                n_loc = jnp.sum(kw, axis=0, keepdims=True)
                c_old = cst_ref[j]
                n_old = nst_ref[j]
                cin_ref[c, j] = c_old.astype(BF16)
                nin_ref[c, j] = n_old
                sp = s_prev[:, j:j + 1]
                sl = s_loc[:, j:j + 1]
                cst_ref[j] = sp * c_old + sl * c_loc
                nst_ref[j] = sp * n_old + sl * n_loc
        return carry

    lax.fori_loop(0, nc, scan, 0)

    def emit(c, carry):
        rows = pl.ds(pl.multiple_of(c * L, L), L)
        bc = b_ref[rows, :]
        rc = r_ref[c]
        t_id = lax.broadcasted_iota(jnp.int32, (L, L), 0)
        s_id = lax.broadcasted_iota(jnp.int32, (L, L), 1)
        masks = (s_id <= t_id, s_id >= t_id)
        m_ins = (minf_ref[c], minb_ref[c])
        for h in range(H):
            q = qs_ref[rows, head(h)]
            qf = q.astype(F32)
            v = v_ref[rows, head(h)]
            s = _dot_nt(q, ks_ref[rows, head(h)])
            intra = None
            num = None
            for d in range(2):
                j = d * H + h
                bcol = bc[:, j:j + 1]
                dmat = jnp.where(masks[d], bcol + rc[j:j + 1, :], NEG_BIG)
                inter = bcol + m_ins[d][:, j:j + 1]
                m_t = jnp.maximum(inter, jnp.max(dmat, axis=-1, keepdims=True))
                sc = s * jnp.exp(dmat - m_t)
                scale_in = jnp.exp(inter - m_t)
                qn = jnp.sum(qf * nin_ref[c, j], axis=-1, keepdims=True)
                den = jnp.sum(sc, axis=-1, keepdims=True) + scale_in * qn
                inv = 1.0 / jnp.maximum(jnp.abs(den), jnp.exp(-m_t))
                part = sc * inv
                intra = part if intra is None else intra + part
                st = _dot((qf * (scale_in * inv)).astype(BF16), cin_ref[c, j])
                num = st if num is None else num + st
            num = num + _dot(intra.astype(BF16), v)
            mu = jnp.mean(num, axis=-1, keepdims=True)
            xc = num - mu
            var = jnp.mean(xc * xc, axis=-1, keepdims=True)
            y = xc * lax.rsqrt(var + NORM_EPS) * ng_ref[:, head(h)] * og_ref[rows, head(h)].astype(F32)
            y_ref[rows, head(h)] = y.astype(BF16)
        return carry

    lax.fori_loop(0, nc, emit, 0)


def _mlstm(qraw, kraw, v, og, gi, gf, grow, conv_w, conv_b, norm_g, batch, seq):
    t = qraw.shape[0]
    nc = seq // MLSTM_CHUNK
    nd = 2 * MLSTM_HEADS
    by_batch = lambda w: pl.BlockSpec((seq, w), lambda b: (b, 0))
    scratch = [
        pltpu.VMEM((seq, MLSTM_WIDTH), BF16),
        pltpu.VMEM((seq, MLSTM_WIDTH), BF16),
        pltpu.VMEM((seq, V7X_LANES), F32),
        pltpu.VMEM((seq, V7X_LANES), F32),
        pltpu.VMEM((nc, 8, MLSTM_CHUNK), F32),
        pltpu.VMEM((nc, 1, V7X_LANES), F32),
        pltpu.VMEM((nc, 1, V7X_LANES), F32),
        pltpu.VMEM((nd, MLSTM_HEAD_DIM, MLSTM_HEAD_DIM), F32),
        pltpu.VMEM((nd, 1, MLSTM_HEAD_DIM), F32),
        pltpu.VMEM((1, V7X_LANES), F32),
        pltpu.VMEM((nc, nd, MLSTM_HEAD_DIM, MLSTM_HEAD_DIM), BF16),
        pltpu.VMEM((nc, nd, 1, MLSTM_HEAD_DIM), F32),
        pltpu.VMEM((nc, 1, V7X_LANES), F32),
        pltpu.VMEM((nc, 1, V7X_LANES), F32),
    ]
    return pl.pallas_call(
        functools.partial(_mlstm_kernel, seq=seq),
        out_shape=jax.ShapeDtypeStruct((t, MLSTM_WIDTH), BF16),
        grid=(batch,),
        in_specs=[by_batch(MLSTM_WIDTH)] * 4 + [by_batch(V7X_LANES)] * 2 + [
            pl.BlockSpec((nc, 16, MLSTM_CHUNK), lambda b: (b, 0, 0)),
            _resident(conv_w.shape), _resident(conv_b.shape), _resident(norm_g.shape)],
        out_specs=by_batch(MLSTM_WIDTH),
        scratch_shapes=scratch,
        compiler_params=_params(1),
        name="bidir_mlstm",
    )(qraw, kraw, v, og, gi, gf, grow, conv_w, conv_b, norm_g)


def _merge_kernel(x_ref, ya_ref, ym_ref, ga_ref, gm_ref, wa_ref, wb_ref, wo_ref, o_ref):
    merged = (ga_ref[...].astype(F32) * _dot(ya_ref[...], wa_ref[...])
              + gm_ref[...].astype(F32) * _dot(ym_ref[...], wb_ref[...]))
    o_ref[...] = x_ref[...] + _dot(merged.astype(BF16), wo_ref[...])


def _merge(x, ya, ym, ga, gm, wa, wb, wo):
    t, d = x.shape
    tm = MERGE_ROWS
    return pl.pallas_call(
        _merge_kernel,
        out_shape=jax.ShapeDtypeStruct((t, d), F32),
        grid=(t // tm,),
        in_specs=[_rows(tm, d), _rows(tm, ATT_WIDTH), _rows(tm, MLSTM_WIDTH), _rows(tm, d), _rows(tm, d),
                  _resident(wa.shape), _resident(wb.shape), _resident(wo.shape)],
        out_specs=_rows(tm, d),
        compiler_params=_params(1),
        name="merge_out",
    )(x, ya, ym, ga, gm, wa, wb, wo)


def _proj_weights(w_in, gate_bias, q_gain, k_gain):
    d = w_in.shape[0]
    H = MLSTM_HEADS
    o_q = ATT_WIDTH
    o_k = o_q + ATT_KV_WIDTH
    o_v = o_k + ATT_KV_WIDTH
    o_qm = o_v + MLSTM_WIDTH
    o_km = o_qm + MLSTM_WIDTH
    o_vm = o_km + MLSTM_WIDTH
    o_om = o_vm + MLSTM_WIDTH
    o_g = o_om + 4 * H
    wg = w_in[:, o_om:o_g]
    order_i = jnp.concatenate([wg[:, 0:H], wg[:, 2 * H:3 * H]], axis=1)
    order_f = jnp.concatenate([wg[:, H:2 * H], wg[:, 3 * H:4 * H]], axis=1)
    pad = jnp.zeros((d, V7X_LANES - 2 * H), w_in.dtype)
    gate_col = jnp.concatenate([order_i, pad, order_f, pad], axis=1)
    gate_row = jnp.concatenate([order_i, order_f], axis=1).T
    b_i = jnp.concatenate([gate_bias[0:H], gate_bias[2 * H:3 * H]])
    b_f = jnp.concatenate([gate_bias[H:2 * H], gate_bias[3 * H:4 * H]])
    bpad = jnp.zeros((V7X_LANES - 2 * H,), F32)
    return {
        "qkv": w_in[:, :o_v].astype(BF16),
        "qkm": w_in[:, o_v:o_km].astype(BF16),
        "vo": w_in[:, o_km:o_om].astype(BF16),
        "gate_col": gate_col.astype(BF16),
        "gate_row": gate_row.astype(BF16),
        "bias_col": jnp.concatenate([b_i, bpad, b_f, bpad]).reshape(1, 2 * V7X_LANES).astype(F32),
        "bias_row": jnp.concatenate([b_i, b_f]).reshape(4 * H, 1).astype(F32),
        "merge": w_in[:, o_g:].astype(BF16),
        "q_gain": jnp.tile(q_gain, ATT_HEADS).reshape(1, ATT_WIDTH).astype(F32),
        "k_gain": jnp.tile(k_gain, ATT_KV_HEADS).reshape(1, ATT_KV_WIDTH).astype(F32),
    }


def kernel(x, positions, ffn1_norm, ffn1_w_gate, ffn1_w_up, ffn1_w_down, mix_norm, w_in, mlstm_gate_bias, attn_q_norm, attn_k_norm, attn_sink, mlstm_conv_w, mlstm_conv_b, mlstm_out_norm, w_branch_attn, w_branch_mlstm, w_out, ffn2_norm, ffn2_w_gate, ffn2_w_up, ffn2_w_down, block_out_norm):
    batch, seq, d = x.shape
    depth = w_in.shape[0]
    t = batch * seq
    xt = x.reshape(t, d)
    cos_t, sin_t = _rope_tables(positions)
    for l in range(depth):
        xt = _ffn(xt, ffn1_norm[l], ffn1_w_gate[l].astype(BF16), ffn1_w_up[l].astype(BF16),
                  ffn1_w_down[l].astype(BF16))
        pw = _proj_weights(w_in[l], mlstm_gate_bias[l], attn_q_norm[l], attn_k_norm[l])
        qa, k4, v4, qm, km, vm, og, gi, gf, grow, ga, gm = _proj(xt, mix_norm[l], pw, cos_t, sin_t)
        ya = _attention(qa, k4, v4, attn_sink[l].astype(F32), batch, seq)
        ym = _mlstm(qm, km, vm, og, gi, gf, grow, mlstm_conv_w[l].astype(F32),
                    mlstm_conv_b[l].reshape(1, -1).astype(F32),
                    mlstm_out_norm[l].reshape(1, -1).astype(F32), batch, seq)
        xt = _merge(xt, ya, ym, ga, gm, w_branch_attn[l].astype(BF16), w_branch_mlstm[l].astype(BF16),
                    w_out[l].astype(BF16))
        xt = _ffn(xt, ffn2_norm[l], ffn2_w_gate[l].astype(BF16), ffn2_w_up[l].astype(BF16),
                  ffn2_w_down[l].astype(BF16), final_gain=block_out_norm[l])
    return xt.reshape(batch, seq, d)
```

```python
import functools

import numpy as np
import jax
import jax.numpy as jnp
from jax import lax
from jax.experimental import pallas as pl
from jax.experimental.pallas import tpu as pltpu

F32 = jnp.float32
BF16 = jnp.bfloat16

ATT_HEAD_DIM = 64
ATT_HEADS = 8
ATT_KV_HEADS = 2
ATT_WIDTH = ATT_HEADS * ATT_HEAD_DIM
ATT_KV_WIDTH = ATT_KV_HEADS * ATT_HEAD_DIM
WINDOW = 128
ATT_BLOCK = 128
ROPE_DIM = ATT_HEAD_DIM // 4
ROPE_THETA = 500000.0
MLSTM_HEADS = 4
MLSTM_HEAD_DIM = 128
MLSTM_WIDTH = MLSTM_HEADS * MLSTM_HEAD_DIM
MLSTM_CHUNK = 128
NORM_EPS = 1e-6
NEG_BIG = -1e30
LOG2E = 1.4426950408889634

V7X_LANES = 128
V7X_MXU_COLS = 256
V7X_VMEM_LIMIT_BYTES = 56 * 1024 * 1024

FFN_ROWS = 1024
PROJ_ROWS = 512
MERGE_ROWS = 1024
ROPE_ROWS = 2048


def _params(n_axes):
    return pltpu.CompilerParams(
        dimension_semantics=("parallel",) * n_axes,
        vmem_limit_bytes=V7X_VMEM_LIMIT_BYTES,
    )


def _resident(shape):
    nd = len(shape)
    return pl.BlockSpec(shape, lambda *_: (0,) * nd, pipeline_mode=pl.Buffered(1))


def _rows(tm, width):
    return pl.BlockSpec((tm, width), lambda i: (i, 0))


def _rms(x, gain):
    ms = jnp.mean(x * x, axis=-1, keepdims=True)
    return x * lax.rsqrt(ms + NORM_EPS) * gain


def _dot(a, b):
    return jnp.dot(a, b, preferred_element_type=F32)


def _dot_nt(a, b):
    return lax.dot_general(a, b, (((1,), (1,)), ((), ())), preferred_element_type=F32)


def _dot_tn(a, b):
    return lax.dot_general(a, b, (((0,), (0,)), ((), ())), preferred_element_type=F32)


def _log_sigmoid(x):
    return jnp.minimum(x, 0.0) - jnp.log1p(jnp.exp(-jnp.abs(x)))


def _rope_kernel(pos_ref, freq_ref, cos_ref, sin_ref):
    ang = pos_ref[...].astype(F32) * freq_ref[...]
    j = lax.broadcasted_iota(jnp.int32, ang.shape, 1) % ATT_HEAD_DIM
    half = ROPE_DIM // 2
    c = jnp.cos(ang)
    s = jnp.sin(ang)
    cos_ref[...] = jnp.where(j < ROPE_DIM, c, 1.0)
    sin_ref[...] = jnp.where(j < half, -s, jnp.where(j < ROPE_DIM, s, 0.0))


def _rope_tables(positions):
    t = positions.size
    half = ROPE_DIM // 2
    inv_freq = np.power(np.float32(ROPE_THETA),
                        -np.arange(half, dtype=np.float32) * np.float32(2.0 / ROPE_DIM)).astype(np.float32)
    lane = np.arange(V7X_LANES) % ATT_HEAD_DIM
    freq = np.where(lane < ROPE_DIM, inv_freq[lane % half], 0.0).astype(np.float32)[None, :]
    pos = positions.reshape(t, 1)
    return pl.pallas_call(
        _rope_kernel,
        out_shape=[jax.ShapeDtypeStruct((t, V7X_LANES), F32)] * 2,
        grid=(t // ROPE_ROWS,),
        in_specs=[_rows(ROPE_ROWS, 1), _resident((1, V7X_LANES))],
        out_specs=[_rows(ROPE_ROWS, V7X_LANES)] * 2,
        compiler_params=_params(1),
        name="rope_tables",
    )(pos, jnp.asarray(freq))


def _ffn_kernel(*refs, d_ff, final_norm):
    if final_norm:
        x_ref, g_ref, wg_ref, wu_ref, wd_ref, go_ref, o_ref, hn_ref, a_ref = refs
    else:
        x_ref, g_ref, wg_ref, wu_ref, wd_ref, o_ref, hn_ref, a_ref = refs
    hn_ref[...] = _rms(x_ref[...], g_ref[...]).astype(BF16)
    for c0 in range(0, d_ff, V7X_MXU_COLS):
        cols = slice(c0, c0 + V7X_MXU_COLS)
        hn = hn_ref[...]
        gate = _dot(hn, wg_ref[:, cols])
        up = _dot(hn, wu_ref[:, cols])
        a_ref[:, cols] = (gate * jax.nn.sigmoid(gate) * up).astype(BF16)
    out = x_ref[...] + 0.5 * _dot(a_ref[...], wd_ref[...])
    if final_norm:
        out = _rms(out, go_ref[...])
    o_ref[...] = out


def _ffn(x, gain, wg, wu, wd, final_gain=None):
    t, d = x.shape
    d_ff = wg.shape[1]
    assert d_ff % V7X_MXU_COLS == 0 and t % FFN_ROWS == 0
    final_norm = final_gain is not None
    args = [x, gain.reshape(1, d), wg, wu, wd]
    in_specs = [_rows(FFN_ROWS, d), _resident((1, d)), _resident((d, d_ff)),
                _resident((d, d_ff)), _resident((d_ff, d))]
    if final_norm:
        args.append(final_gain.reshape(1, d))
        in_specs.append(_resident((1, d)))
    return pl.pallas_call(
        functools.partial(_ffn_kernel, d_ff=d_ff, final_norm=final_norm),
        out_shape=jax.ShapeDtypeStruct((t, d), F32),
        grid=(t // FFN_ROWS,),
        in_specs=in_specs,
        out_specs=_rows(FFN_ROWS, d),
        scratch_shapes=[pltpu.VMEM((FFN_ROWS, d), BF16), pltpu.VMEM((FFN_ROWS, d_ff), BF16)],
        compiler_params=_params(1),
        name="ffn_final" if final_norm else "ffn",
    )(*args)


def _head_rms(x, ones_bd, gain):
    sq = x * x
    hi = sq.astype(BF16)
    lo = (sq - hi.astype(F32)).astype(BF16)
    ss = _dot(hi, ones_bd) + _dot(lo, ones_bd)
    return x * lax.rsqrt(ss * (1.0 / ATT_HEAD_DIM) + NORM_EPS) * gain


def _rope(x, cos, sin):
    n = x.shape[1]
    half = ROPE_DIM // 2
    j = lax.broadcasted_iota(jnp.int32, x.shape, 1) % ATT_HEAD_DIM
    partner = jnp.where(j < half, pltpu.roll(x, n - half, 1), pltpu.roll(x, half, 1))
    return x * cos + partner * sin


def _lo_hi(x):
    lo = lax.broadcasted_iota(jnp.int32, x.shape, 1) < ATT_HEAD_DIM
    xr = pltpu.roll(x, ATT_HEAD_DIM, 1)
    zero = jnp.zeros_like(x)
    return jnp.concatenate([jnp.where(lo, x, zero), jnp.where(lo, zero, xr),
                            jnp.where(lo, xr, zero), jnp.where(lo, zero, x)], axis=1)


def _proj_kernel(x_ref, g_ref, wqkv_ref, wqkm_ref, wvo_ref, wgc_ref, wgr_ref, bgc_ref, bgr_ref,
                 wmg_ref, qg_ref, kg_ref, cos_ref, sin_ref, ones_ref,
                 qa_ref, k4_ref, v4_ref, qm_ref, km_ref, vm_ref, og_ref, gi_ref, gf_ref, gr_ref,
                 ga_ref, gmm_ref, hn_ref):
    hn_ref[...] = _rms(x_ref[...], g_ref[...]).astype(BF16)
    hn = hn_ref[...]
    cos = cos_ref[...]
    sin = sin_ref[...]
    ones_bd = ones_ref[...]

    q = _dot(hn, wqkv_ref[:, :ATT_WIDTH])
    q = _head_rms(q, ones_bd, qg_ref[...])
    reps = ATT_WIDTH // V7X_LANES
    q = _rope(q, jnp.concatenate([cos] * reps, axis=1), jnp.concatenate([sin] * reps, axis=1))
    qa_ref[...] = (q * (ATT_HEAD_DIM ** -0.5 * LOG2E)).astype(BF16)
    k = _dot(hn, wqkv_ref[:, ATT_WIDTH:ATT_WIDTH + ATT_KV_WIDTH])
    k = _head_rms(k, ones_bd[:ATT_KV_WIDTH, :ATT_KV_WIDTH], kg_ref[...])
    k4_ref[...] = _lo_hi(_rope(k, cos, sin)).astype(BF16)
    v = _dot(hn, wqkv_ref[:, ATT_WIDTH + ATT_KV_WIDTH:])
    v4_ref[...] = _lo_hi(v).astype(BF16)

    qm_ref[...] = _dot(hn, wqkm_ref[:, :MLSTM_WIDTH]).astype(BF16)
    km_ref[...] = _dot(hn, wqkm_ref[:, MLSTM_WIDTH:]).astype(BF16)
    vm_ref[...] = _dot(hn, wvo_ref[:, :MLSTM_WIDTH]).astype(BF16)
    og_ref[...] = jax.nn.sigmoid(_dot(hn, wvo_ref[:, MLSTM_WIDTH:])).astype(BF16)
    gc = _dot(hn, wgc_ref[...]) + bgc_ref[...]
    gi_ref[...] = gc[:, :V7X_LANES]
    gf_ref[...] = gc[:, V7X_LANES:]
    gr = _dot_nt(wgr_ref[...], hn) + bgr_ref[...]
    for c in range(gr_ref.shape[0]):
        gr_ref[c] = gr[:, c * MLSTM_CHUNK:(c + 1) * MLSTM_CHUNK]

    d = ga_ref.shape[1]
    ga_ref[...] = jax.nn.sigmoid(_dot(hn, wmg_ref[:, :d])).astype(BF16)
    gmm_ref[...] = jax.nn.sigmoid(_dot(hn, wmg_ref[:, d:])).astype(BF16)


def _proj(x, gain, w, cos_t, sin_t):
    t, d = x.shape
    tm = PROJ_ROWS
    nck = tm // MLSTM_CHUNK
    heads = np.arange(ATT_WIDTH) // ATT_HEAD_DIM
    ones_bd = jnp.asarray((heads[:, None] == heads[None, :]).astype(np.float32), dtype=BF16)
    out_shape = [
        jax.ShapeDtypeStruct((t, ATT_WIDTH), BF16),
        jax.ShapeDtypeStruct((t, 4 * V7X_LANES), BF16),
        jax.ShapeDtypeStruct((t, 4 * V7X_LANES), BF16),
        jax.ShapeDtypeStruct((t, MLSTM_WIDTH), BF16),
        jax.ShapeDtypeStruct((t, MLSTM_WIDTH), BF16),
        jax.ShapeDtypeStruct((t, MLSTM_WIDTH), BF16),
        jax.ShapeDtypeStruct((t, MLSTM_WIDTH), BF16),
        jax.ShapeDtypeStruct((t, V7X_LANES), F32),
        jax.ShapeDtypeStruct((t, V7X_LANES), F32),
        jax.ShapeDtypeStruct((t // MLSTM_CHUNK, 16, MLSTM_CHUNK), F32),
        jax.ShapeDtypeStruct((t, d), BF16),
        jax.ShapeDtypeStruct((t, d), BF16),
    ]
    out_specs = [
        _rows(tm, ATT_WIDTH), _rows(tm, 4 * V7X_LANES), _rows(tm, 4 * V7X_LANES),
        _rows(tm, MLSTM_WIDTH), _rows(tm, MLSTM_WIDTH), _rows(tm, MLSTM_WIDTH), _rows(tm, MLSTM_WIDTH),
        _rows(tm, V7X_LANES), _rows(tm, V7X_LANES),
        pl.BlockSpec((nck, 16, MLSTM_CHUNK), lambda i: (i, 0, 0)),
        _rows(tm, d), _rows(tm, d),
    ]
    args = [x, gain.reshape(1, d), w["qkv"], w["qkm"], w["vo"], w["gate_col"], w["gate_row"],
            w["bias_col"], w["bias_row"], w["merge"], w["q_gain"], w["k_gain"], cos_t, sin_t, ones_bd]
    in_specs = [_rows(tm, d)] + [_resident(a.shape) for a in args[1:12]] + [
        _rows(tm, V7X_LANES), _rows(tm, V7X_LANES), _resident(ones_bd.shape)]
    return pl.pallas_call(
        _proj_kernel,
        out_shape=out_shape,
        grid=(t // tm,),
        in_specs=in_specs,
        out_specs=out_specs,
        scratch_shapes=[pltpu.VMEM((tm, d), BF16)],
        compiler_params=_params(1),
        name="mixer_proj",
    )(*args)


def _attn_kernel(sink_ref, q_ref, k4_ref, v4_ref, o_ref, *, seq):
    blk = ATT_BLOCK
    assert WINDOW == blk
    nb = seq // blk
    group = ATT_HEADS // ATT_KV_HEADS

    def body(n, carry):
        r0 = pl.multiple_of(n * blk, blk)
        left = pl.multiple_of(jnp.maximum(r0 - blk, 0), blk)
        right = pl.multiple_of(jnp.minimum(r0 + blk, seq - blk), blk)
        qb = q_ref[pl.ds(r0, blk), :]
        kb = jnp.concatenate([k4_ref[pl.ds(left, blk), :], k4_ref[pl.ds(r0, blk), :],
                              k4_ref[pl.ds(right, blk), :]], axis=0)
        vb = jnp.concatenate([v4_ref[pl.ds(left, blk), :], v4_ref[pl.ds(r0, blk), :],
                              v4_ref[pl.ds(right, blk), :]], axis=0)
        ql = lax.broadcasted_iota(jnp.int32, (blk, blk), 0)
        kl = lax.broadcasted_iota(jnp.int32, (blk, blk), 1)
        left_ok = kl >= ql + jnp.where(n > 0, 0, blk)
        right_ok = kl + jnp.where(n < nb - 1, 0, blk) <= ql
        lo = lax.broadcasted_iota(jnp.int32, (blk, V7X_LANES), 1) < ATT_HEAD_DIM
        lo_band = lax.broadcasted_iota(jnp.int32, (3 * blk, V7X_LANES), 1) < ATT_HEAD_DIM
        ones_lo = jnp.where(lo_band, 1.0, 0.0).astype(BF16)
        ones_hi = jnp.where(lo_band, 0.0, 1.0).astype(BF16)
        outs = []
        for g in range(ATT_KV_HEADS):
            kv_lo = slice(2 * g * V7X_LANES, (2 * g + 1) * V7X_LANES)
            kv_hi = slice((2 * g + 1) * V7X_LANES, (2 * g + 2) * V7X_LANES)
            v_aug = (jnp.concatenate([vb[:, kv_lo], ones_lo], axis=1),
                     jnp.concatenate([vb[:, kv_hi], ones_hi], axis=1))
            for p in range(group // 2):
                pair = g * (group // 2) + p
                qp = qb[:, pair * V7X_LANES:(pair + 1) * V7X_LANES]
                acc = None
                shifts = []
                for half, cols in enumerate((kv_lo, kv_hi)):
                    sink = sink_ref[2 * pair + half] * LOG2E
                    s = _dot_nt(qp, kb[:, cols])
                    s_l = jnp.where(left_ok, s[:, :blk], NEG_BIG)
                    s_m = s[:, blk:2 * blk]
                    s_r = jnp.where(right_ok, s[:, 2 * blk:], NEG_BIG)
                    m = jnp.max(jnp.maximum(jnp.maximum(s_l, s_r), s_m), axis=-1, keepdims=True)
                    m = jnp.maximum(m, sink)
                    e = jnp.concatenate([jnp.exp2(s_l - m), jnp.exp2(s_m - m), jnp.exp2(s_r - m)], axis=1)
                    part = _dot(e.astype(BF16), v_aug[half])
                    acc = part if acc is None else acc + part
                    shifts.append(sink - m)
                den = acc[:, V7X_LANES:] + jnp.exp2(jnp.where(lo, shifts[0], shifts[1]))
                outs.append(acc[:, :V7X_LANES] / den)
        o_ref[pl.ds(r0, blk), :] = jnp.concatenate(outs, axis=1).astype(BF16)
        return carry

    lax.fori_loop(0, nb, body, 0)


def _attention(q, k4, v4, sink, batch, seq):
    t = q.shape[0]
    by_batch = lambda w: pl.BlockSpec((seq, w), lambda b: (b, 0))
    return pl.pallas_call(
        functools.partial(_attn_kernel, seq=seq),
        out_shape=jax.ShapeDtypeStruct((t, ATT_WIDTH), BF16),
        grid=(batch,),
        in_specs=[pl.BlockSpec(memory_space=pltpu.SMEM), by_batch(ATT_WIDTH),
                  by_batch(4 * V7X_LANES), by_batch(4 * V7X_LANES)],
        out_specs=by_batch(ATT_WIDTH),
        compiler_params=_params(1),
        name="window_attn",
    )(sink, q, k4, v4)


def _scan(x, axis, reverse, op, identity):
    n = x.shape[axis]
    idx = lax.broadcasted_iota(jnp.int32, x.shape, axis)
    d = 1
    while d < n:
        if reverse:
            x = op(x, jnp.where(idx < n - d, pltpu.roll(x, n - d, axis), identity))
        else:
            x = op(x, jnp.where(idx >= d, pltpu.roll(x, d, axis), identity))
        d *= 2
    return x


def _split_mean(x, mean_mat):
    hi = x.astype(BF16)
    lo = (x - hi.astype(F32)).astype(BF16)
    return _dot(hi, mean_mat) + _dot(lo, mean_mat)


def _mlstm_kernel(qraw_ref, kraw_ref, v_ref, og_ref, gi_ref, gf_ref, gr_ref, cw_ref, cb_ref, ng_ref,
                  y_ref,
                  qs_ref, ks_ref, b_ref, dm_ref, w_ref, r_ref, am_ref, bt_ref,
                  cst_ref, mst_ref, cin_ref, minf_ref, minb_ref, *, seq):
    L = MLSTM_CHUNK
    H = MLSTM_HEADS
    dh = MLSTM_HEAD_DIM
    nc = seq // L
    width = MLSTM_WIDTH
    halo = 16

    def head(h):
        return slice(h * dh, (h + 1) * dh)

    def with_ones(v):
        return jnp.concatenate([v, jnp.ones_like(v)], axis=1)

    def prep(c, carry):
        r0 = pl.multiple_of(c * L, L)
        rows = pl.ds(r0, L)
        rid = lax.broadcasted_iota(jnp.int32, (L, width), 0)
        has_prev = (c > 0).astype(F32)
        has_next = (c < nc - 1).astype(F32)
        prev_at = pl.multiple_of(jnp.maximum(r0 - halo, 0), halo)
        next_at = pl.multiple_of(jnp.minimum(r0 + L, seq - halo), halo)
        for idx, (raw_ref, dst_ref) in enumerate(((qraw_ref, qs_ref), (kraw_ref, ks_ref))):
            cols = slice(idx * width, (idx + 1) * width)
            cur = raw_ref[rows, :].astype(F32)
            prev_row = raw_ref[pl.ds(prev_at, halo), :].astype(F32)[halo - 1:halo, :] * has_prev
            next_row = raw_ref[pl.ds(next_at, halo), :].astype(F32)[0:1, :] * has_next
            before = jnp.where(rid == 0, prev_row, pltpu.roll(cur, 1, 0))
            after = jnp.where(rid == L - 1, next_row, pltpu.roll(cur, L - 1, 0))
            u = (before * cw_ref[0:1, cols] + cur * cw_ref[1:2, cols] + after * cw_ref[2:3, cols]
                 + cb_ref[:, cols])
            u = u * jax.nn.sigmoid(u)
            if idx == 1:
                u = u * (dh ** -0.5)
            dst_ref[rows, :] = u.astype(BF16)

        lane = lax.broadcasted_iota(jnp.int32, (L, V7X_LANES), 1)
        gi = gi_ref[rows, :]
        lf = _log_sigmoid(gf_ref[rows, :])
        fwd = lane < H
        bc = jnp.where(fwd, _scan(lf, 0, False, jnp.add, 0.0), _scan(lf, 0, True, jnp.add, 0.0))
        bt = jnp.where(fwd[0:1], bc[L - 1:L, :], bc[0:1, :])
        a = bt - bc + gi
        am = jnp.max(a, axis=0, keepdims=True)
        rr = gi - bc
        dm = bc + jnp.where(fwd, _scan(rr, 0, False, jnp.maximum, NEG_BIG),
                            _scan(rr, 0, True, jnp.maximum, NEG_BIG))
        b_ref[rows, :] = bc
        dm_ref[rows, :] = dm
        w_ref[rows, :] = jnp.exp(a - am)
        am_ref[c] = am
        bt_ref[c] = bt

        g_rows = gr_ref[c]
        lfr = _log_sigmoid(g_rows[8:16, :])
        sub = lax.broadcasted_iota(jnp.int32, (8, L), 0)
        br = jnp.where(sub < H, _scan(lfr, 1, False, jnp.add, 0.0), _scan(lfr, 1, True, jnp.add, 0.0))
        r_ref[c] = g_rows[0:8, :] - br
        return carry

    lax.fori_loop(0, nc, prep, 0)

    cst_ref[...] = jnp.zeros_like(cst_ref)
    mst_ref[...] = jnp.zeros_like(mst_ref)

    def scan(i, carry):
        c_f = i
        c_b = nc - 1 - i
        lane1 = lax.broadcasted_iota(jnp.int32, (1, V7X_LANES), 1)
        fwd = lane1 < H
        bt = jnp.where(fwd, bt_ref[c_f], bt_ref[c_b])
        am = jnp.where(fwd, am_ref[c_f], am_ref[c_b])
        m_old = mst_ref[...]
        m_new = jnp.maximum(bt + m_old, am)
        s_prev = jnp.exp(bt + m_old - m_new)
        s_loc = jnp.exp(am - m_new)
        minf_ref[c_f] = m_old
        minb_ref[c_b] = m_old
        mst_ref[...] = m_new
        for d, c in enumerate((c_f, c_b)):
            rows = pl.ds(pl.multiple_of(c * L, L), L)
            wc = w_ref[rows, :]
            for h in range(H):
                j = d * H + h
                kw = ks_ref[rows, head(h)].astype(F32) * wc[:, j:j + 1]
                c_loc = _dot_tn(kw.astype(BF16), with_ones(v_ref[rows, head(h)]))
                c_old = cst_ref[j]
                cin_ref[c, j] = c_old.astype(BF16)
                cst_ref[j] = s_prev[:, j:j + 1] * c_old + s_loc[:, j:j + 1] * c_loc
        return carry

    lax.fori_loop(0, nc, scan, 0)

    def emit(c, carry):
        rows = pl.ds(pl.multiple_of(c * L, L), L)
        bc = b_ref[rows, :]
        rc = r_ref[c]
        lane1 = lax.broadcasted_iota(jnp.int32, (1, V7X_LANES), 1)
        m_in = jnp.where(lane1 < H, minf_ref[c], minb_ref[c])
        m_t = jnp.maximum(bc + m_in, dm_ref[rows, :])
        t_id = lax.broadcasted_iota(jnp.int32, (L, L), 0)
        s_id = lax.broadcasted_iota(jnp.int32, (L, L), 1)
        masks = (s_id <= t_id, s_id >= t_id)
        mean_mat = jnp.full((dh, dh), 1.0 / dh, BF16)
        for h in range(H):
            q = qs_ref[rows, head(h)]
            v1 = with_ones(v_ref[rows, head(h)])
            s = _dot_nt(q, ks_ref[rows, head(h)])
            hsum = None
            for d in range(2):
                j = d * H + h
                bcol = jnp.broadcast_to(bc[:, j:j + 1], (L, L))
                mcol = jnp.broadcast_to(m_t[:, j:j + 1], (L, L))
                p = jnp.exp(jnp.where(masks[d], bcol + rc[j:j + 1, :], NEG_BIG) - mcol)
                scale_in = jnp.exp(bcol + m_in[:, j:j + 1] - mcol)
                intra = _dot((s * p).astype(BF16), v1)
                inter = _dot(q, cin_ref[c, j])
                num = intra[:, :dh] + scale_in * inter[:, :dh]
                den = intra[:, dh:] + scale_in * inter[:, dh:]
                hd = num / jnp.maximum(jnp.abs(den), jnp.exp(-mcol))
                hsum = hd if hsum is None else hsum + hd
            xc = hsum - _split_mean(hsum, mean_mat)
            var = _split_mean(xc * xc, mean_mat)
            y = xc * lax.rsqrt(var + NORM_EPS) * ng_ref[:, head(h)] * og_ref[rows, head(h)].astype(F32)
            y_ref[rows, head(h)] = y.astype(BF16)
        return carry

    lax.fori_loop(0, nc, emit, 0)


def _mlstm(qraw, kraw, v, og, gi, gf, grow, conv_w, conv_b, norm_g, batch, seq):
    t = qraw.shape[0]
    nc = seq // MLSTM_CHUNK
    nd = 2 * MLSTM_HEADS
    by_batch = lambda w: pl.BlockSpec((seq, w), lambda b: (b, 0))
    scratch = [
        pltpu.VMEM((seq, MLSTM_WIDTH), BF16),
        pltpu.VMEM((seq, MLSTM_WIDTH), BF16),
        pltpu.VMEM((seq, V7X_LANES), F32),
        pltpu.VMEM((seq, V7X_LANES), F32),
        pltpu.VMEM((seq, V7X_LANES), F32),
        pltpu.VMEM((nc, 8, MLSTM_CHUNK), F32),
        pltpu.VMEM((nc, 1, V7X_LANES), F32),
        pltpu.VMEM((nc, 1, V7X_LANES), F32),
        pltpu.VMEM((nd, MLSTM_HEAD_DIM, 2 * MLSTM_HEAD_DIM), F32),
        pltpu.VMEM((1, V7X_LANES), F32),
        pltpu.VMEM((nc, nd, MLSTM_HEAD_DIM, 2 * MLSTM_HEAD_DIM), BF16),
        pltpu.VMEM((nc, 1, V7X_LANES), F32),
        pltpu.VMEM((nc, 1, V7X_LANES), F32),
    ]
    return pl.pallas_call(
        functools.partial(_mlstm_kernel, seq=seq),
        out_shape=jax.ShapeDtypeStruct((t, MLSTM_WIDTH), BF16),
        grid=(batch,),
        in_specs=[by_batch(MLSTM_WIDTH)] * 4 + [by_batch(V7X_LANES)] * 2 + [
            pl.BlockSpec((nc, 16, MLSTM_CHUNK), lambda b: (b, 0, 0)),
            _resident(conv_w.shape), _resident(conv_b.shape), _resident(norm_g.shape)],
        out_specs=by_batch(MLSTM_WIDTH),
        scratch_shapes=scratch,
        compiler_params=_params(1),
        name="bidir_mlstm",
    )(qraw, kraw, v, og, gi, gf, grow, conv_w, conv_b, norm_g)


def _merge_kernel(x_ref, ya_ref, ym_ref, ga_ref, gm_ref, wa_ref, wb_ref, wo_ref, o_ref):
    merged = (ga_ref[...].astype(F32) * _dot(ya_ref[...], wa_ref[...])
              + gm_ref[...].astype(F32) * _dot(ym_ref[...], wb_ref[...]))
    o_ref[...] = x_ref[...] + _dot(merged.astype(BF16), wo_ref[...])


def _merge(x, ya, ym, ga, gm, wa, wb, wo):
    t, d = x.shape
    tm = MERGE_ROWS
    return pl.pallas_call(
        _merge_kernel,
        out_shape=jax.ShapeDtypeStruct((t, d), F32),
        grid=(t // tm,),
        in_specs=[_rows(tm, d), _rows(tm, ATT_WIDTH), _rows(tm, MLSTM_WIDTH), _rows(tm, d), _rows(tm, d),
                  _resident(wa.shape), _resident(wb.shape), _resident(wo.shape)],
        out_specs=_rows(tm, d),
        compiler_params=_params(1),
        name="merge_out",
    )(x, ya, ym, ga, gm, wa, wb, wo)


def _proj_weights(w_in, gate_bias, q_gain, k_gain):
    d = w_in.shape[0]
    H = MLSTM_HEADS
    o_q = ATT_WIDTH
    o_k = o_q + ATT_KV_WIDTH
    o_v = o_k + ATT_KV_WIDTH
    o_qm = o_v + MLSTM_WIDTH
    o_km = o_qm + MLSTM_WIDTH
    o_vm = o_km + MLSTM_WIDTH
    o_om = o_vm + MLSTM_WIDTH
    o_g = o_om + 4 * H
    wg = w_in[:, o_om:o_g]
    order_i = jnp.concatenate([wg[:, 0:H], wg[:, 2 * H:3 * H]], axis=1)
    order_f = jnp.concatenate([wg[:, H:2 * H], wg[:, 3 * H:4 * H]], axis=1)
    pad = jnp.zeros((d, V7X_LANES - 2 * H), w_in.dtype)
    gate_col = jnp.concatenate([order_i, pad, order_f, pad], axis=1)
    gate_row = jnp.concatenate([order_i, order_f], axis=1).T
    b_i = jnp.concatenate([gate_bias[0:H], gate_bias[2 * H:3 * H]])
    b_f = jnp.concatenate([gate_bias[H:2 * H], gate_bias[3 * H:4 * H]])
    bpad = jnp.zeros((V7X_LANES - 2 * H,), F32)
    return {
        "qkv": w_in[:, :o_v].astype(BF16),
        "qkm": w_in[:, o_v:o_km].astype(BF16),
        "vo": w_in[:, o_km:o_om].astype(BF16),
        "gate_col": gate_col.astype(BF16),
        "gate_row": gate_row.astype(BF16),
        "bias_col": jnp.concatenate([b_i, bpad, b_f, bpad]).reshape(1, 2 * V7X_LANES).astype(F32),
        "bias_row": jnp.concatenate([b_i, b_f]).reshape(4 * H, 1).astype(F32),
        "merge": w_in[:, o_g:].astype(BF16),
        "q_gain": jnp.tile(q_gain, ATT_HEADS).reshape(1, ATT_WIDTH).astype(F32),
        "k_gain": jnp.tile(k_gain, ATT_KV_HEADS).reshape(1, ATT_KV_WIDTH).astype(F32),
    }


def kernel(x, positions, ffn1_norm, ffn1_w_gate, ffn1_w_up, ffn1_w_down, mix_norm, w_in, mlstm_gate_bias, attn_q_norm, attn_k_norm, attn_sink, mlstm_conv_w, mlstm_conv_b, mlstm_out_norm, w_branch_attn, w_branch_mlstm, w_out, ffn2_norm, ffn2_w_gate, ffn2_w_up, ffn2_w_down, block_out_norm):
    batch, seq, d = x.shape
    depth = w_in.shape[0]
    t = batch * seq
    xt = x.reshape(t, d)
    cos_t, sin_t = _rope_tables(positions)
    for l in range(depth):
        xt = _ffn(xt, ffn1_norm[l], ffn1_w_gate[l].astype(BF16), ffn1_w_up[l].astype(BF16),
                  ffn1_w_down[l].astype(BF16))
        pw = _proj_weights(w_in[l], mlstm_gate_bias[l], attn_q_norm[l], attn_k_norm[l])
        qa, k4, v4, qm, km, vm, og, gi, gf, grow, ga, gm = _proj(xt, mix_norm[l], pw, cos_t, sin_t)
        ya = _attention(qa, k4, v4, attn_sink[l].astype(F32), batch, seq)
        ym = _mlstm(qm, km, vm, og, gi, gf, grow, mlstm_conv_w[l].astype(F32),
                    mlstm_conv_b[l].reshape(1, -1).astype(F32),
                    mlstm_out_norm[l].reshape(1, -1).astype(F32), batch, seq)
        xt = _merge(xt, ya, ym, ga, gm, w_branch_attn[l].astype(BF16), w_branch_mlstm[l].astype(BF16),
                    w_out[l].astype(BF16))
        xt = _ffn(xt, ffn2_norm[l], ffn2_w_gate[l].astype(BF16), ffn2_w_up[l].astype(BF16),
                  ffn2_w_down[l].astype(BF16), final_gain=block_out_norm[l])
    return xt.reshape(batch, seq, d)
```

```python
import functools

import numpy as np
import jax
import jax.numpy as jnp
from jax import lax
from jax.experimental import pallas as pl
from jax.experimental.pallas import tpu as pltpu

F32 = jnp.float32
BF16 = jnp.bfloat16

ATT_HEAD_DIM = 64
ATT_HEADS = 8
ATT_KV_HEADS = 2
ATT_WIDTH = ATT_HEADS * ATT_HEAD_DIM
ATT_KV_WIDTH = ATT_KV_HEADS * ATT_HEAD_DIM
WINDOW = 128
ATT_BLOCK = 128
ROPE_DIM = ATT_HEAD_DIM // 4
ROPE_THETA = 500000.0
MLSTM_HEADS = 4
MLSTM_HEAD_DIM = 128
MLSTM_WIDTH = MLSTM_HEADS * MLSTM_HEAD_DIM
MLSTM_CHUNK = 128
NORM_EPS = 1e-6
NEG_BIG = -1e30
LOG2E = 1.4426950408889634

V7X_LANES = 128
V7X_MXU_COLS = 256
V7X_VMEM_LIMIT_BYTES = 56 * 1024 * 1024

FFN_ROWS = 1024
PROJ_ROWS = 512
MERGE_ROWS = 1024
ROPE_ROWS = 2048


def _params(n_axes):
    return pltpu.CompilerParams(
        dimension_semantics=("parallel",) * n_axes,
        vmem_limit_bytes=V7X_VMEM_LIMIT_BYTES,
    )


def _resident(shape):
    nd = len(shape)
    return pl.BlockSpec(shape, lambda *_: (0,) * nd, pipeline_mode=pl.Buffered(1))


def _rows(tm, width):
    return pl.BlockSpec((tm, width), lambda i: (i, 0))


def _rms(x, gain):
    ms = jnp.mean(x * x, axis=-1, keepdims=True)
    return x * lax.rsqrt(ms + NORM_EPS) * gain


def _dot(a, b):
    return jnp.dot(a, b, preferred_element_type=F32)


def _dot_nt(a, b):
    return lax.dot_general(a, b, (((1,), (1,)), ((), ())), preferred_element_type=F32)


def _dot_tn(a, b):
    return lax.dot_general(a, b, (((0,), (0,)), ((), ())), preferred_element_type=F32)


def _log_sigmoid(x):
    return jnp.minimum(x, 0.0) - jnp.log1p(jnp.exp(-jnp.abs(x)))


def _rope_kernel(pos_ref, freq_ref, cos_ref, sin_ref):
    ang = pos_ref[...].astype(F32) * freq_ref[...]
    j = lax.broadcasted_iota(jnp.int32, ang.shape, 1) % ATT_HEAD_DIM
    half = ROPE_DIM // 2
    c = jnp.cos(ang)
    s = jnp.sin(ang)
    cos_ref[...] = jnp.where(j < ROPE_DIM, c, 1.0)
    sin_ref[...] = jnp.where(j < half, -s, jnp.where(j < ROPE_DIM, s, 0.0))


def _rope_tables(positions):
    t = positions.size
    half = ROPE_DIM // 2
    inv_freq = np.power(np.float32(ROPE_THETA),
                        -np.arange(half, dtype=np.float32) * np.float32(2.0 / ROPE_DIM)).astype(np.float32)
    lane = np.arange(V7X_LANES) % ATT_HEAD_DIM
    freq = np.where(lane < ROPE_DIM, inv_freq[lane % half], 0.0).astype(np.float32)[None, :]
    pos = positions.reshape(t, 1)
    return pl.pallas_call(
        _rope_kernel,
        out_shape=[jax.ShapeDtypeStruct((t, V7X_LANES), F32)] * 2,
        grid=(t // ROPE_ROWS,),
        in_specs=[_rows(ROPE_ROWS, 1), _resident((1, V7X_LANES))],
        out_specs=[_rows(ROPE_ROWS, V7X_LANES)] * 2,
        compiler_params=_params(1),
        name="rope_tables",
    )(pos, jnp.asarray(freq))


def _ffn_kernel(*refs, d_ff, final_norm):
    if final_norm:
        x_ref, g_ref, wg_ref, wu_ref, wd_ref, go_ref, o_ref, hn_ref, a_ref = refs
    else:
        x_ref, g_ref, wg_ref, wu_ref, wd_ref, o_ref, hn_ref, a_ref = refs
    hn_ref[...] = _rms(x_ref[...], g_ref[...]).astype(BF16)
    for c0 in range(0, d_ff, V7X_MXU_COLS):
        cols = slice(c0, c0 + V7X_MXU_COLS)
        hn = hn_ref[...]
        gate = _dot(hn, wg_ref[:, cols])
        up = _dot(hn, wu_ref[:, cols])
        a_ref[:, cols] = (gate * jax.nn.sigmoid(gate) * up).astype(BF16)
    out = x_ref[...] + 0.5 * _dot(a_ref[...], wd_ref[...])
    if final_norm:
        out = _rms(out, go_ref[...])
    o_ref[...] = out


def _ffn(x, gain, wg, wu, wd, final_gain=None):
    t, d = x.shape
    d_ff = wg.shape[1]
    assert d_ff % V7X_MXU_COLS == 0 and t % FFN_ROWS == 0
    final_norm = final_gain is not None
    args = [x, gain.reshape(1, d), wg, wu, wd]
    in_specs = [_rows(FFN_ROWS, d), _resident((1, d)), _resident((d, d_ff)),
                _resident((d, d_ff)), _resident((d_ff, d))]
    if final_norm:
        args.append(final_gain.reshape(1, d))
        in_specs.append(_resident((1, d)))
    return pl.pallas_call(
        functools.partial(_ffn_kernel, d_ff=d_ff, final_norm=final_norm),
        out_shape=jax.ShapeDtypeStruct((t, d), F32),
        grid=(t // FFN_ROWS,),
        in_specs=in_specs,
        out_specs=_rows(FFN_ROWS, d),
        scratch_shapes=[pltpu.VMEM((FFN_ROWS, d), BF16), pltpu.VMEM((FFN_ROWS, d_ff), BF16)],
        compiler_params=_params(1),
        name="ffn_final" if final_norm else "ffn",
    )(*args)


def _head_rms(x, ones_bd, gain):
    sq = x * x
    hi = sq.astype(BF16)
    lo = (sq - hi.astype(F32)).astype(BF16)
    ss = _dot(hi, ones_bd) + _dot(lo, ones_bd)
    return x * lax.rsqrt(ss * (1.0 / ATT_HEAD_DIM) + NORM_EPS) * gain


def _rope(x, cos, sin):
    n = x.shape[1]
    half = ROPE_DIM // 2
    j = lax.broadcasted_iota(jnp.int32, x.shape, 1) % ATT_HEAD_DIM
    partner = jnp.where(j < half, pltpu.roll(x, n - half, 1), pltpu.roll(x, half, 1))
    return x * cos + partner * sin


def _lo_hi(x):
    lo = lax.broadcasted_iota(jnp.int32, x.shape, 1) < ATT_HEAD_DIM
    xr = pltpu.roll(x, ATT_HEAD_DIM, 1)
    zero = jnp.zeros_like(x)
    return jnp.concatenate([jnp.where(lo, x, zero), jnp.where(lo, zero, xr),
                            jnp.where(lo, xr, zero), jnp.where(lo, zero, x)], axis=1)


def _proj_kernel(x_ref, g_ref, wqkv_ref, wqkm_ref, wvo_ref, wgr_ref, bgr_ref,
                 wmg_ref, qg_ref, kg_ref, cos_ref, sin_ref, ones_ref,
                 qa_ref, k4_ref, v4_ref, qm_ref, km_ref, vm_ref, og_ref, gr_ref,
                 ga_ref, gmm_ref, hn_ref):
    hn_ref[...] = _rms(x_ref[...], g_ref[...]).astype(BF16)
    hn = hn_ref[...]
    cos = cos_ref[...]
    sin = sin_ref[...]
    ones_bd = ones_ref[...]

    q = _dot(hn, wqkv_ref[:, :ATT_WIDTH])
    q = _head_rms(q, ones_bd, qg_ref[...])
    reps = ATT_WIDTH // V7X_LANES
    q = _rope(q, jnp.concatenate([cos] * reps, axis=1), jnp.concatenate([sin] * reps, axis=1))
    qa_ref[...] = (q * (ATT_HEAD_DIM ** -0.5 * LOG2E)).astype(BF16)
    k = _dot(hn, wqkv_ref[:, ATT_WIDTH:ATT_WIDTH + ATT_KV_WIDTH])
    k = _head_rms(k, ones_bd[:ATT_KV_WIDTH, :ATT_KV_WIDTH], kg_ref[...])
    k4_ref[...] = _lo_hi(_rope(k, cos, sin)).astype(BF16)
    v = _dot(hn, wqkv_ref[:, ATT_WIDTH + ATT_KV_WIDTH:])
    v4_ref[...] = _lo_hi(v).astype(BF16)

    qm_ref[...] = _dot(hn, wqkm_ref[:, :MLSTM_WIDTH]).astype(BF16)
    km_ref[...] = _dot(hn, wqkm_ref[:, MLSTM_WIDTH:]).astype(BF16)
    vm_ref[...] = _dot(hn, wvo_ref[:, :MLSTM_WIDTH]).astype(BF16)
    og_ref[...] = jax.nn.sigmoid(_dot(hn, wvo_ref[:, MLSTM_WIDTH:])).astype(BF16)
    gr = _dot_nt(wgr_ref[...], hn) + bgr_ref[...]
    for c in range(gr_ref.shape[0]):
        gr_ref[c] = gr[:, c * MLSTM_CHUNK:(c + 1) * MLSTM_CHUNK]

    d = ga_ref.shape[1]
    ga_ref[...] = jax.nn.sigmoid(_dot(hn, wmg_ref[:, :d])).astype(BF16)
    gmm_ref[...] = jax.nn.sigmoid(_dot(hn, wmg_ref[:, d:])).astype(BF16)


def _proj(x, gain, w, cos_t, sin_t):
    t, d = x.shape
    tm = PROJ_ROWS
    nck = tm // MLSTM_CHUNK
    heads = np.arange(ATT_WIDTH) // ATT_HEAD_DIM
    ones_bd = jnp.asarray((heads[:, None] == heads[None, :]).astype(np.float32), dtype=BF16)
    out_shape = [
        jax.ShapeDtypeStruct((t, ATT_WIDTH), BF16),
        jax.ShapeDtypeStruct((t, 4 * V7X_LANES), BF16),
        jax.ShapeDtypeStruct((t, 4 * V7X_LANES), BF16),
        jax.ShapeDtypeStruct((t, MLSTM_WIDTH), BF16),
        jax.ShapeDtypeStruct((t, MLSTM_WIDTH), BF16),
        jax.ShapeDtypeStruct((t, MLSTM_WIDTH), BF16),
        jax.ShapeDtypeStruct((t, MLSTM_WIDTH), BF16),
        jax.ShapeDtypeStruct((t // MLSTM_CHUNK, 16, MLSTM_CHUNK), F32),
        jax.ShapeDtypeStruct((t, d), BF16),
        jax.ShapeDtypeStruct((t, d), BF16),
    ]
    out_specs = [
        _rows(tm, ATT_WIDTH), _rows(tm, 4 * V7X_LANES), _rows(tm, 4 * V7X_LANES),
        _rows(tm, MLSTM_WIDTH), _rows(tm, MLSTM_WIDTH), _rows(tm, MLSTM_WIDTH), _rows(tm, MLSTM_WIDTH),
        pl.BlockSpec((nck, 16, MLSTM_CHUNK), lambda i: (i, 0, 0)),
        _rows(tm, d), _rows(tm, d),
    ]
    args = [x, gain.reshape(1, d), w["qkv"], w["qkm"], w["vo"], w["gate_row"],
            w["bias_row"], w["merge"], w["q_gain"], w["k_gain"], cos_t, sin_t, ones_bd]
    in_specs = [_rows(tm, d)] + [_resident(a.shape) for a in args[1:10]] + [
        _rows(tm, V7X_LANES), _rows(tm, V7X_LANES), _resident(ones_bd.shape)]
    return pl.pallas_call(
        _proj_kernel,
        out_shape=out_shape,
        grid=(t // tm,),
        in_specs=in_specs,
        out_specs=out_specs,
        scratch_shapes=[pltpu.VMEM((tm, d), BF16)],
        compiler_params=_params(1),
        name="mixer_proj",
    )(*args)


def _attn_kernel(sink_ref, q_ref, k4_ref, v4_ref, o_ref, *, seq):
    blk = ATT_BLOCK
    assert WINDOW == blk
    nb = seq // blk
    group = ATT_HEADS // ATT_KV_HEADS

    def body(n, carry):
        r0 = pl.multiple_of(n * blk, blk)
        left = pl.multiple_of(jnp.maximum(r0 - blk, 0), blk)
        right = pl.multiple_of(jnp.minimum(r0 + blk, seq - blk), blk)
        qb = q_ref[pl.ds(r0, blk), :]
        kb = jnp.concatenate([k4_ref[pl.ds(left, blk), :], k4_ref[pl.ds(r0, blk), :],
                              k4_ref[pl.ds(right, blk), :]], axis=0)
        vb = jnp.concatenate([v4_ref[pl.ds(left, blk), :], v4_ref[pl.ds(r0, blk), :],
                              v4_ref[pl.ds(right, blk), :]], axis=0)
        ql = lax.broadcasted_iota(jnp.int32, (blk, blk), 0)
        kl = lax.broadcasted_iota(jnp.int32, (blk, blk), 1)
        left_ok = kl >= ql + jnp.where(n > 0, 0, blk)
        right_ok = kl + jnp.where(n < nb - 1, 0, blk) <= ql
        lo = lax.broadcasted_iota(jnp.int32, (blk, V7X_LANES), 1) < ATT_HEAD_DIM
        lo_band = lax.broadcasted_iota(jnp.int32, (3 * blk, V7X_LANES), 1) < ATT_HEAD_DIM
        ones_lo = jnp.where(lo_band, 1.0, 0.0).astype(BF16)
        ones_hi = jnp.where(lo_band, 0.0, 1.0).astype(BF16)
        outs = []
        for g in range(ATT_KV_HEADS):
            kv_lo = slice(2 * g * V7X_LANES, (2 * g + 1) * V7X_LANES)
            kv_hi = slice((2 * g + 1) * V7X_LANES, (2 * g + 2) * V7X_LANES)
            v_aug = jnp.concatenate([jnp.concatenate([vb[:, kv_lo], ones_lo], axis=1),
                                     jnp.concatenate([vb[:, kv_hi], ones_hi], axis=1)], axis=0)
            k_both = jnp.concatenate([kb[:, kv_lo], kb[:, kv_hi]], axis=0)
            for p in range(group // 2):
                pair = g * (group // 2) + p
                qp = qb[:, pair * V7X_LANES:(pair + 1) * V7X_LANES]
                s_both = _dot_nt(qp, k_both)
                probs, shifts = [], []
                for half in range(2):
                    sink = sink_ref[2 * pair + half] * LOG2E
                    s = s_both[:, half * 3 * blk:(half + 1) * 3 * blk]
                    s_l = jnp.where(left_ok, s[:, :blk], NEG_BIG)
                    s_m = s[:, blk:2 * blk]
                    s_r = jnp.where(right_ok, s[:, 2 * blk:], NEG_BIG)
                    m = jnp.max(jnp.maximum(jnp.maximum(s_l, s_r), s_m), axis=-1, keepdims=True)
                    m = jnp.maximum(m, sink)
                    probs += [jnp.exp2(s_l - m), jnp.exp2(s_m - m), jnp.exp2(s_r - m)]
                    shifts.append(sink - m)
                acc = _dot(jnp.concatenate(probs, axis=1).astype(BF16), v_aug)
                den = acc[:, V7X_LANES:] + jnp.exp2(jnp.where(lo, shifts[0], shifts[1]))
                outs.append(acc[:, :V7X_LANES] / den)
        o_ref[pl.ds(r0, blk), :] = jnp.concatenate(outs, axis=1).astype(BF16)
        return carry

    lax.fori_loop(0, nb, body, 0, unroll=4)


def _attention(q, k4, v4, sink, batch, seq):
    t = q.shape[0]
    by_batch = lambda w: pl.BlockSpec((seq, w), lambda b: (b, 0))
    return pl.pallas_call(
        functools.partial(_attn_kernel, seq=seq),
        out_shape=jax.ShapeDtypeStruct((t, ATT_WIDTH), BF16),
        grid=(batch,),
        in_specs=[pl.BlockSpec(memory_space=pltpu.SMEM), by_batch(ATT_WIDTH),
                  by_batch(4 * V7X_LANES), by_batch(4 * V7X_LANES)],
        out_specs=by_batch(ATT_WIDTH),
        compiler_params=_params(1),
        name="window_attn",
    )(sink, q, k4, v4)


def _scan(x, axis, reverse, op, identity):
    n = x.shape[axis]
    idx = lax.broadcasted_iota(jnp.int32, x.shape, axis)
    d = 1
    while d < n:
        if reverse:
            x = op(x, jnp.where(idx < n - d, pltpu.roll(x, n - d, axis), identity))
        else:
            x = op(x, jnp.where(idx >= d, pltpu.roll(x, d, axis), identity))
        d *= 2
    return x


def _mlstm_kernel(qraw_ref, kraw_ref, v_ref, og_ref, gr_ref, cw_ref, cb_ref, ng_ref,
                  y_ref,
                  qs_ref, ks_ref, b_ref, dm_ref, w_ref, r_ref, am_ref, bt_ref,
                  cst_ref, mst_ref, cin_ref, minf_ref, minb_ref, hs_ref, *, seq):
    L = MLSTM_CHUNK
    H = MLSTM_HEADS
    dh = MLSTM_HEAD_DIM
    nc = seq // L
    width = MLSTM_WIDTH
    halo = 16

    def head(h):
        return slice(h * dh, (h + 1) * dh)

    def with_ones(v):
        return jnp.concatenate([v, jnp.ones_like(v)], axis=1)

    assert nc * 8 == V7X_LANES and L == V7X_LANES
    shape = (nc * 8, L)
    gi = gr_ref[:, 0:8, :].reshape(shape)
    lf = _log_sigmoid(gr_ref[:, 8:16, :].reshape(shape))
    fwd = (lax.broadcasted_iota(jnp.int32, shape, 0) & 7) < H
    lane = lax.broadcasted_iota(jnp.int32, shape, 1)
    b = jnp.where(fwd, _scan(lf, 1, False, jnp.add, 0.0), _scan(lf, 1, True, jnp.add, 0.0))
    b_tot = jnp.sum(jnp.where(lane == jnp.where(fwd, L - 1, 0), b, 0.0), axis=1, keepdims=True)
    a = b_tot - b + gi
    a_max = jnp.max(a, axis=1, keepdims=True)
    r = gi - b
    d_max = b + jnp.where(fwd, _scan(r, 1, False, jnp.maximum, NEG_BIG),
                          _scan(r, 1, True, jnp.maximum, NEG_BIG))
    r_ref[...] = (r * LOG2E).reshape(nc, 8, L)
    cols = ((b * LOG2E).T, (d_max * LOG2E).T, jnp.exp(a - a_max).T)
    stats = (jnp.broadcast_to(a_max, shape).T[0:1, :], jnp.broadcast_to(b_tot, shape).T[0:1, :])
    for c in range(nc):
        shift = (V7X_LANES - 8 * c) % V7X_LANES
        for src, dst_ref in zip(cols, (b_ref, dm_ref, w_ref)):
            dst_ref[c * L:(c + 1) * L, :] = pltpu.roll(src, shift, 1) if shift else src
        for src, dst_ref in zip(stats, (am_ref, bt_ref)):
            dst_ref[c] = pltpu.roll(src, shift, 1) if shift else src

    def prep(c, carry):
        r0 = pl.multiple_of(c * L, L)
        rows = pl.ds(r0, L)
        rid = lax.broadcasted_iota(jnp.int32, (L, width), 0)
        has_prev = jnp.where(c > 0, 1.0, 0.0)
        has_next = jnp.where(c < nc - 1, 1.0, 0.0)
        prev_at = pl.multiple_of(jnp.maximum(r0 - halo, 0), halo)
        next_at = pl.multiple_of(jnp.minimum(r0 + L, seq - halo), halo)
        for idx, (raw_ref, dst_ref) in enumerate(((qraw_ref, qs_ref), (kraw_ref, ks_ref))):
            cols = slice(idx * width, (idx + 1) * width)
            cur = raw_ref[rows, :].astype(F32)
            prev_row = raw_ref[pl.ds(prev_at, halo), :].astype(F32)[halo - 1:halo, :] * has_prev
            next_row = raw_ref[pl.ds(next_at, halo), :].astype(F32)[0:1, :] * has_next
            before = jnp.where(rid == 0, prev_row, pltpu.roll(cur, 1, 0))
            after = jnp.where(rid == L - 1, next_row, pltpu.roll(cur, L - 1, 0))
            u = (before * cw_ref[0:1, cols] + cur * cw_ref[1:2, cols] + after * cw_ref[2:3, cols]
                 + cb_ref[:, cols])
            u = u * jax.nn.sigmoid(u)
            if idx == 1:
                u = u * (dh ** -0.5)
            dst_ref[rows, :] = u.astype(BF16)
        return carry

    lax.fori_loop(0, nc, prep, 0)

    cst_ref[...] = jnp.zeros_like(cst_ref)
    mst_ref[...] = jnp.zeros_like(mst_ref)

    def scan(i, carry):
        c_f = i
        c_b = nc - 1 - i
        lane1 = lax.broadcasted_iota(jnp.int32, (1, V7X_LANES), 1)
        fwd = lane1 < H
        bt = jnp.where(fwd, bt_ref[c_f], bt_ref[c_b])
        am = jnp.where(fwd, am_ref[c_f], am_ref[c_b])
        m_old = mst_ref[...]
        m_new = jnp.maximum(bt + m_old, am)
        s_prev = jnp.exp(bt + m_old - m_new)
        s_loc = jnp.exp(am - m_new)
        minf_ref[c_f] = m_old
        minb_ref[c_b] = m_old
        mst_ref[...] = m_new
        for d, c in enumerate((c_f, c_b)):
            rows = pl.ds(pl.multiple_of(c * L, L), L)
            wc = w_ref[rows, :]
            for h in range(H):
                j = d * H + h
                kw = ks_ref[rows, head(h)].astype(F32) * wc[:, j:j + 1]
                c_loc = _dot_tn(kw.astype(BF16), with_ones(v_ref[rows, head(h)]))
                c_old = cst_ref[j]
                cin_ref[c, j] = c_old.astype(BF16)
                cst_ref[j] = s_prev[:, j:j + 1] * c_old + s_loc[:, j:j + 1] * c_loc
        return carry

    lax.fori_loop(0, nc, scan, 0)

    def emit(c):
        rows = pl.ds(pl.multiple_of(c * L, L), L)
        bc = b_ref[rows, :]
        rc = r_ref[c]
        lane1 = lax.broadcasted_iota(jnp.int32, (1, V7X_LANES), 1)
        m_in = jnp.where(lane1 < H, minf_ref[c], minb_ref[c]) * LOG2E
        m_t = jnp.maximum(bc + m_in, dm_ref[rows, :])
        b_rel = bc - m_t
        t_id = lax.broadcasted_iota(jnp.int32, (L, L), 0)
        s_id = lax.broadcasted_iota(jnp.int32, (L, L), 1)
        masks = (s_id <= t_id, s_id >= t_id)
        for h in range(H):
            q = qs_ref[rows, head(h)]
            qf = q.astype(F32)
            v1 = with_ones(v_ref[rows, head(h)])
            s = _dot_nt(q, ks_ref[rows, head(h)])
            hsum = None
            for d in range(2):
                j = d * H + h
                bcol = jnp.broadcast_to(b_rel[:, j:j + 1], (L, L))
                mcol = jnp.broadcast_to(m_t[:, j:j + 1], (L, L))
                p = jnp.exp2(jnp.where(masks[d], bcol + rc[j:j + 1, :], NEG_BIG))
                scale_in = jnp.exp2(bcol + m_in[:, j:j + 1])
                lhs = jnp.concatenate([(s * p).astype(BF16), (scale_in * qf).astype(BF16)], axis=1)
                tot = _dot(lhs, jnp.concatenate([v1, cin_ref[c, j]], axis=0))
                hd = tot[:, :dh] / jnp.maximum(jnp.abs(tot[:, dh:]), jnp.exp2(-mcol))
                hsum = hd if hsum is None else hsum + hd
            hs_ref[rows, head(h)] = hsum

    def finish(c):
        rows = pl.ds(pl.multiple_of(c * L, L), L)
        for h in range(H):
            hsum = hs_ref[rows, head(h)]
            xc = hsum - jnp.mean(hsum, axis=-1, keepdims=True)
            var = jnp.mean(xc * xc, axis=-1, keepdims=True)
            y = xc * lax.rsqrt(var + NORM_EPS) * ng_ref[:, head(h)] * og_ref[rows, head(h)].astype(F32)
            y_ref[rows, head(h)] = y.astype(BF16)

    def emit_and_finish(c, carry):
        finish(c - 1)
        emit(c)
        return carry

    emit(0)
    lax.fori_loop(1, nc, emit_and_finish, 0)
    finish(nc - 1)


def _mlstm(qraw, kraw, v, og, grow, conv_w, conv_b, norm_g, batch, seq):
    t = qraw.shape[0]
    nc = seq // MLSTM_CHUNK
    nd = 2 * MLSTM_HEADS
    by_batch = lambda w: pl.BlockSpec((seq, w), lambda b: (b, 0))
    scratch = [
        pltpu.VMEM((seq, MLSTM_WIDTH), BF16),
        pltpu.VMEM((seq, MLSTM_WIDTH), BF16),
        pltpu.VMEM((seq, V7X_LANES), F32),
        pltpu.VMEM((seq, V7X_LANES), F32),
        pltpu.VMEM((seq, V7X_LANES), F32),
        pltpu.VMEM((nc, 8, MLSTM_CHUNK), F32),
        pltpu.VMEM((nc, 1, V7X_LANES), F32),
        pltpu.VMEM((nc, 1, V7X_LANES), F32),
        pltpu.VMEM((nd, MLSTM_HEAD_DIM, 2 * MLSTM_HEAD_DIM), F32),
        pltpu.VMEM((1, V7X_LANES), F32),
        pltpu.VMEM((nc, nd, MLSTM_HEAD_DIM, 2 * MLSTM_HEAD_DIM), BF16),
        pltpu.VMEM((nc, 1, V7X_LANES), F32),
        pltpu.VMEM((nc, 1, V7X_LANES), F32),
        pltpu.VMEM((seq, MLSTM_WIDTH), F32),
    ]
    return pl.pallas_call(
        functools.partial(_mlstm_kernel, seq=seq),
        out_shape=jax.ShapeDtypeStruct((t, MLSTM_WIDTH), BF16),
        grid=(batch,),
        in_specs=[by_batch(MLSTM_WIDTH)] * 4 + [
            pl.BlockSpec((nc, 16, MLSTM_CHUNK), lambda b: (b, 0, 0)),
            _resident(conv_w.shape), _resident(conv_b.shape), _resident(norm_g.shape)],
        out_specs=by_batch(MLSTM_WIDTH),
        scratch_shapes=scratch,
        compiler_params=_params(1),
        name="bidir_mlstm",
    )(qraw, kraw, v, og, grow, conv_w, conv_b, norm_g)


def _merge_kernel(x_ref, ya_ref, ym_ref, ga_ref, gm_ref, wa_ref, wb_ref, wo_ref, o_ref):
    merged = (ga_ref[...].astype(F32) * _dot(ya_ref[...], wa_ref[...])
              + gm_ref[...].astype(F32) * _dot(ym_ref[...], wb_ref[...]))
    o_ref[...] = x_ref[...] + _dot(merged.astype(BF16), wo_ref[...])


def _merge(x, ya, ym, ga, gm, wa, wb, wo):
    t, d = x.shape
    tm = MERGE_ROWS
    return pl.pallas_call(
        _merge_kernel,
        out_shape=jax.ShapeDtypeStruct((t, d), F32),
        grid=(t // tm,),
        in_specs=[_rows(tm, d), _rows(tm, ATT_WIDTH), _rows(tm, MLSTM_WIDTH), _rows(tm, d), _rows(tm, d),
                  _resident(wa.shape), _resident(wb.shape), _resident(wo.shape)],
        out_specs=_rows(tm, d),
        compiler_params=_params(1),
        name="merge_out",
    )(x, ya, ym, ga, gm, wa, wb, wo)


def _proj_weights(w_in, gate_bias, q_gain, k_gain):
    d = w_in.shape[0]
    H = MLSTM_HEADS
    o_q = ATT_WIDTH
    o_k = o_q + ATT_KV_WIDTH
    o_v = o_k + ATT_KV_WIDTH
    o_qm = o_v + MLSTM_WIDTH
    o_km = o_qm + MLSTM_WIDTH
    o_vm = o_km + MLSTM_WIDTH
    o_om = o_vm + MLSTM_WIDTH
    o_g = o_om + 4 * H
    wg = w_in[:, o_om:o_g]
    order_i = jnp.concatenate([wg[:, 0:H], wg[:, 2 * H:3 * H]], axis=1)
    order_f = jnp.concatenate([wg[:, H:2 * H], wg[:, 3 * H:4 * H]], axis=1)
    gate_row = jnp.concatenate([order_i, order_f], axis=1).T
    b_i = jnp.concatenate([gate_bias[0:H], gate_bias[2 * H:3 * H]])
    b_f = jnp.concatenate([gate_bias[H:2 * H], gate_bias[3 * H:4 * H]])
    return {
        "qkv": w_in[:, :o_v].astype(BF16),
        "qkm": w_in[:, o_v:o_km].astype(BF16),
        "vo": w_in[:, o_km:o_om].astype(BF16),
        "gate_row": gate_row.astype(BF16),
        "bias_row": jnp.concatenate([b_i, b_f]).reshape(4 * H, 1).astype(F32),
        "merge": w_in[:, o_g:].astype(BF16),
        "q_gain": jnp.tile(q_gain, ATT_HEADS).reshape(1, ATT_WIDTH).astype(F32),
        "k_gain": jnp.tile(k_gain, ATT_KV_HEADS).reshape(1, ATT_KV_WIDTH).astype(F32),
    }


def kernel(x, positions, ffn1_norm, ffn1_w_gate, ffn1_w_up, ffn1_w_down, mix_norm, w_in, mlstm_gate_bias, attn_q_norm, attn_k_norm, attn_sink, mlstm_conv_w, mlstm_conv_b, mlstm_out_norm, w_branch_attn, w_branch_mlstm, w_out, ffn2_norm, ffn2_w_gate, ffn2_w_up, ffn2_w_down, block_out_norm):
    batch, seq, d = x.shape
    depth = w_in.shape[0]
    t = batch * seq
    xt = x.reshape(t, d)
    cos_t, sin_t = _rope_tables(positions)
    for l in range(depth):
        xt = _ffn(xt, ffn1_norm[l], ffn1_w_gate[l].astype(BF16), ffn1_w_up[l].astype(BF16),
                  ffn1_w_down[l].astype(BF16))
        pw = _proj_weights(w_in[l], mlstm_gate_bias[l], attn_q_norm[l], attn_k_norm[l])
        qa, k4, v4, qm, km, vm, og, grow, ga, gm = _proj(xt, mix_norm[l], pw, cos_t, sin_t)
        ya = _attention(qa, k4, v4, attn_sink[l].astype(F32), batch, seq)
        ym = _mlstm(qm, km, vm, og, grow, mlstm_conv_w[l].astype(F32),
                    mlstm_conv_b[l].reshape(1, -1).astype(F32),
                    mlstm_out_norm[l].reshape(1, -1).astype(F32), batch, seq)
        xt = _merge(xt, ya, ym, ga, gm, w_branch_attn[l].astype(BF16), w_branch_mlstm[l].astype(BF16),
                    w_out[l].astype(BF16))
        xt = _ffn(xt, ffn2_norm[l], ffn2_w_gate[l].astype(BF16), ffn2_w_up[l].astype(BF16),
                  ffn2_w_down[l].astype(BF16), final_gain=block_out_norm[l])
    return xt.reshape(batch, seq, d)
```

```python
import functools

import numpy as np
import jax
import jax.numpy as jnp
from jax import lax
from jax.experimental import pallas as pl
from jax.experimental.pallas import tpu as pltpu

F32 = jnp.float32
BF16 = jnp.bfloat16

ATT_HEAD_DIM = 64
ATT_HEADS = 8
ATT_KV_HEADS = 2
ATT_WIDTH = ATT_HEADS * ATT_HEAD_DIM
ATT_KV_WIDTH = ATT_KV_HEADS * ATT_HEAD_DIM
WINDOW = 128
ATT_BLOCK = 128
ROPE_DIM = ATT_HEAD_DIM // 4
ROPE_THETA = 500000.0
MLSTM_HEADS = 4
MLSTM_HEAD_DIM = 128
MLSTM_WIDTH = MLSTM_HEADS * MLSTM_HEAD_DIM
MLSTM_CHUNK = 128
NORM_EPS = 1e-6
NEG_BIG = -1e30
LOG2E = 1.4426950408889634

V7X_LANES = 128
V7X_MXU_COLS = 256
V7X_VMEM_LIMIT_BYTES = 56 * 1024 * 1024

FFN_ROWS = 1024
PROJ_ROWS = 512
MERGE_ROWS = 1024
ROPE_ROWS = 2048


def _params(n_axes):
    return pltpu.CompilerParams(
        dimension_semantics=("parallel",) * n_axes,
        vmem_limit_bytes=V7X_VMEM_LIMIT_BYTES,
    )


def _resident(shape):
    nd = len(shape)
    return pl.BlockSpec(shape, lambda *_: (0,) * nd, pipeline_mode=pl.Buffered(1))


def _rows(tm, width):
    return pl.BlockSpec((tm, width), lambda i: (i, 0))


def _rms(x, gain):
    ms = jnp.mean(x * x, axis=-1, keepdims=True)
    return x * lax.rsqrt(ms + NORM_EPS) * gain


def _dot(a, b):
    return jnp.dot(a, b, preferred_element_type=F32)


def _dot_nt(a, b):
    return lax.dot_general(a, b, (((1,), (1,)), ((), ())), preferred_element_type=F32)


def _dot_tn(a, b):
    return lax.dot_general(a, b, (((0,), (0,)), ((), ())), preferred_element_type=F32)


def _log_sigmoid(x):
    return jnp.minimum(x, 0.0) - jnp.log1p(jnp.exp(-jnp.abs(x)))


def _rope_kernel(pos_ref, freq_ref, cos_ref, sin_ref):
    ang = pos_ref[...].astype(F32) * freq_ref[...]
    j = lax.broadcasted_iota(jnp.int32, ang.shape, 1) % ATT_HEAD_DIM
    half = ROPE_DIM // 2
    c = jnp.cos(ang)
    s = jnp.sin(ang)
    cos_ref[...] = jnp.where(j < ROPE_DIM, c, 1.0)
    sin_ref[...] = jnp.where(j < half, -s, jnp.where(j < ROPE_DIM, s, 0.0))


def _rope_tables(positions):
    t = positions.size
    half = ROPE_DIM // 2
    inv_freq = np.power(np.float32(ROPE_THETA),
                        -np.arange(half, dtype=np.float32) * np.float32(2.0 / ROPE_DIM)).astype(np.float32)
    lane = np.arange(V7X_LANES) % ATT_HEAD_DIM
    freq = np.where(lane < ROPE_DIM, inv_freq[lane % half], 0.0).astype(np.float32)[None, :]
    pos = positions.reshape(t, 1)
    return pl.pallas_call(
        _rope_kernel,
        out_shape=[jax.ShapeDtypeStruct((t, V7X_LANES), F32)] * 2,
        grid=(t // ROPE_ROWS,),
        in_specs=[_rows(ROPE_ROWS, 1), _resident((1, V7X_LANES))],
        out_specs=[_rows(ROPE_ROWS, V7X_LANES)] * 2,
        compiler_params=_params(1),
        name="rope_tables",
    )(pos, jnp.asarray(freq))


def _ffn_kernel(*refs, d_ff, final_norm):
    if final_norm:
        x_ref, g_ref, wg_ref, wu_ref, wd_ref, go_ref, o_ref, hn_ref, a_ref = refs
    else:
        x_ref, g_ref, wg_ref, wu_ref, wd_ref, o_ref, hn_ref, a_ref = refs
    hn_ref[...] = _rms(x_ref[...], g_ref[...]).astype(BF16)
    for c0 in range(0, d_ff, V7X_MXU_COLS):
        cols = slice(c0, c0 + V7X_MXU_COLS)
        hn = hn_ref[...]
        gate = _dot(hn, wg_ref[:, cols])
        up = _dot(hn, wu_ref[:, cols])
        a_ref[:, cols] = (gate * jax.nn.sigmoid(gate) * up).astype(BF16)
    out = x_ref[...] + 0.5 * _dot(a_ref[...], wd_ref[...])
    if final_norm:
        out = _rms(out, go_ref[...])
    o_ref[...] = out


def _ffn(x, gain, wg, wu, wd, final_gain=None):
    t, d = x.shape
    d_ff = wg.shape[1]
    assert d_ff % V7X_MXU_COLS == 0 and t % FFN_ROWS == 0
    final_norm = final_gain is not None
    args = [x, gain.reshape(1, d), wg, wu, wd]
    in_specs = [_rows(FFN_ROWS, d), _resident((1, d)), _resident((d, d_ff)),
                _resident((d, d_ff)), _resident((d_ff, d))]
    if final_norm:
        args.append(final_gain.reshape(1, d))
        in_specs.append(_resident((1, d)))
    return pl.pallas_call(
        functools.partial(_ffn_kernel, d_ff=d_ff, final_norm=final_norm),
        out_shape=jax.ShapeDtypeStruct((t, d), F32),
        grid=(t // FFN_ROWS,),
        in_specs=in_specs,
        out_specs=_rows(FFN_ROWS, d),
        scratch_shapes=[pltpu.VMEM((FFN_ROWS, d), BF16), pltpu.VMEM((FFN_ROWS, d_ff), BF16)],
        compiler_params=_params(1),
        name="ffn_final" if final_norm else "ffn",
    )(*args)


def _head_rms(x, ones_bd, gain):
    ss = _dot((x * x).astype(BF16), ones_bd)
    return x * lax.rsqrt(ss * (1.0 / ATT_HEAD_DIM) + NORM_EPS) * gain


def _rope(x, cos, sin):
    n = x.shape[1]
    half = ROPE_DIM // 2
    j = lax.broadcasted_iota(jnp.int32, x.shape, 1) % ATT_HEAD_DIM
    partner = jnp.where(j < half, pltpu.roll(x, n - half, 1), pltpu.roll(x, half, 1))
    return x * cos + partner * sin


def _lo_hi(x):
    lo = lax.broadcasted_iota(jnp.int32, x.shape, 1) < ATT_HEAD_DIM
    xr = pltpu.roll(x, ATT_HEAD_DIM, 1)
    zero = jnp.zeros_like(x)
    return jnp.concatenate([jnp.where(lo, x, zero), jnp.where(lo, zero, xr),
                            jnp.where(lo, xr, zero), jnp.where(lo, zero, x)], axis=1)


def _proj_kernel(x_ref, g_ref, wqkv_ref, wqkm_ref, wvo_ref, wgr_ref, bgr_ref,
                 wmg_ref, qg_ref, kg_ref, cos_ref, sin_ref, ones_ref,
                 qa_ref, k4_ref, v4_ref, qm_ref, km_ref, vm_ref, og_ref, gr_ref,
                 ga_ref, gmm_ref, hn_ref):
    hn_ref[...] = _rms(x_ref[...], g_ref[...]).astype(BF16)
    hn = hn_ref[...]
    cos = cos_ref[...]
    sin = sin_ref[...]
    ones_bd = ones_ref[...]

    q = _dot(hn, wqkv_ref[:, :ATT_WIDTH])
    k = _dot(hn, wqkv_ref[:, ATT_WIDTH:ATT_WIDTH + ATT_KV_WIDTH])
    v = _dot(hn, wqkv_ref[:, ATT_WIDTH + ATT_KV_WIDTH:])
    qm_ref[...] = _dot(hn, wqkm_ref[:, :MLSTM_WIDTH]).astype(BF16)
    q = _head_rms(q, ones_bd, qg_ref[...])
    km_ref[...] = _dot(hn, wqkm_ref[:, MLSTM_WIDTH:]).astype(BF16)
    k = _head_rms(k, ones_bd[:ATT_KV_WIDTH, :ATT_KV_WIDTH], kg_ref[...])
    vm_ref[...] = _dot(hn, wvo_ref[:, :MLSTM_WIDTH]).astype(BF16)
    og_ref[...] = jax.nn.sigmoid(_dot(hn, wvo_ref[:, MLSTM_WIDTH:])).astype(BF16)
    gr = _dot_nt(wgr_ref[...], hn) + bgr_ref[...]
    for c in range(gr_ref.shape[0]):
        gr_ref[c] = gr[:, c * MLSTM_CHUNK:(c + 1) * MLSTM_CHUNK]

    reps = ATT_WIDTH // V7X_LANES
    q = _rope(q, jnp.concatenate([cos] * reps, axis=1), jnp.concatenate([sin] * reps, axis=1))
    qa_ref[...] = (q * (ATT_HEAD_DIM ** -0.5 * LOG2E)).astype(BF16)
    k4_ref[...] = _lo_hi(_rope(k, cos, sin)).astype(BF16)
    v4_ref[...] = _lo_hi(v).astype(BF16)

    d = ga_ref.shape[1]
    piece = 2 * V7X_MXU_COLS
    for c0 in range(0, d, piece):
        ga_ref[:, c0:c0 + piece] = jax.nn.sigmoid(_dot(hn, wmg_ref[:, c0:c0 + piece])).astype(BF16)
    for c0 in range(0, d, piece):
        gmm_ref[:, c0:c0 + piece] = jax.nn.sigmoid(_dot(hn, wmg_ref[:, d + c0:d + c0 + piece])).astype(BF16)


def _proj(x, gain, w, cos_t, sin_t):
    t, d = x.shape
    tm = PROJ_ROWS
    nck = tm // MLSTM_CHUNK
    heads = np.arange(ATT_WIDTH) // ATT_HEAD_DIM
    ones_bd = jnp.asarray((heads[:, None] == heads[None, :]).astype(np.float32), dtype=BF16)
    out_shape = [
        jax.ShapeDtypeStruct((t, ATT_WIDTH), BF16),
        jax.ShapeDtypeStruct((t, 4 * V7X_LANES), BF16),
        jax.ShapeDtypeStruct((t, 4 * V7X_LANES), BF16),
        jax.ShapeDtypeStruct((t, MLSTM_WIDTH), BF16),
        jax.ShapeDtypeStruct((t, MLSTM_WIDTH), BF16),
        jax.ShapeDtypeStruct((t, MLSTM_WIDTH), BF16),
        jax.ShapeDtypeStruct((t, MLSTM_WIDTH), BF16),
        jax.ShapeDtypeStruct((t // MLSTM_CHUNK, 16, MLSTM_CHUNK), F32),
        jax.ShapeDtypeStruct((t, d), BF16),
        jax.ShapeDtypeStruct((t, d), BF16),
    ]
    out_specs = [
        _rows(tm, ATT_WIDTH), _rows(tm, 4 * V7X_LANES), _rows(tm, 4 * V7X_LANES),
        _rows(tm, MLSTM_WIDTH), _rows(tm, MLSTM_WIDTH), _rows(tm, MLSTM_WIDTH), _rows(tm, MLSTM_WIDTH),
        pl.BlockSpec((nck, 16, MLSTM_CHUNK), lambda i: (i, 0, 0)),
        _rows(tm, d), _rows(tm, d),
    ]
    args = [x, gain.reshape(1, d), w["qkv"], w["qkm"], w["vo"], w["gate_row"],
            w["bias_row"], w["merge"], w["q_gain"], w["k_gain"], cos_t, sin_t, ones_bd]
    in_specs = [_rows(tm, d)] + [_resident(a.shape) for a in args[1:10]] + [
        _rows(tm, V7X_LANES), _rows(tm, V7X_LANES), _resident(ones_bd.shape)]
    return pl.pallas_call(
        _proj_kernel,
        out_shape=out_shape,
        grid=(t // tm,),
        in_specs=in_specs,
        out_specs=out_specs,
        scratch_shapes=[pltpu.VMEM((tm, d), BF16)],
        compiler_params=_params(1),
        name="mixer_proj",
    )(*args)


def _attn_kernel(sink_ref, q_ref, k4_ref, v4_ref, o_ref, *, seq):
    blk = ATT_BLOCK
    assert WINDOW == blk
    nb = seq // blk
    group = ATT_HEADS // ATT_KV_HEADS

    def body(n, carry):
        r0 = pl.multiple_of(n * blk, blk)
        left = pl.multiple_of(jnp.maximum(r0 - blk, 0), blk)
        right = pl.multiple_of(jnp.minimum(r0 + blk, seq - blk), blk)
        qb = q_ref[pl.ds(r0, blk), :]
        kb = jnp.concatenate([k4_ref[pl.ds(left, blk), :], k4_ref[pl.ds(r0, blk), :],
                              k4_ref[pl.ds(right, blk), :]], axis=0)
        vb = jnp.concatenate([v4_ref[pl.ds(left, blk), :], v4_ref[pl.ds(r0, blk), :],
                              v4_ref[pl.ds(right, blk), :]], axis=0)
        ql = lax.broadcasted_iota(jnp.int32, (blk, blk), 0)
        kl = lax.broadcasted_iota(jnp.int32, (blk, blk), 1)
        left_ok = kl >= ql + jnp.where(n > 0, 0, blk)
        right_ok = kl + jnp.where(n < nb - 1, 0, blk) <= ql
        lo = lax.broadcasted_iota(jnp.int32, (blk, V7X_LANES), 1) < ATT_HEAD_DIM
        lo_band = lax.broadcasted_iota(jnp.int32, (3 * blk, V7X_LANES), 1) < ATT_HEAD_DIM
        ones_lo = jnp.where(lo_band, 1.0, 0.0).astype(BF16)
        ones_hi = jnp.where(lo_band, 0.0, 1.0).astype(BF16)
        outs = []
        for g in range(ATT_KV_HEADS):
            kv_lo = slice(2 * g * V7X_LANES, (2 * g + 1) * V7X_LANES)
            kv_hi = slice((2 * g + 1) * V7X_LANES, (2 * g + 2) * V7X_LANES)
            v_aug = jnp.concatenate([jnp.concatenate([vb[:, kv_lo], ones_lo], axis=1),
                                     jnp.concatenate([vb[:, kv_hi], ones_hi], axis=1)], axis=0)
            k_both = jnp.concatenate([kb[:, kv_lo], kb[:, kv_hi]], axis=0)
            for p in range(group // 2):
                pair = g * (group // 2) + p
                qp = qb[:, pair * V7X_LANES:(pair + 1) * V7X_LANES]
                s_both = _dot_nt(qp, k_both)
                probs, shifts = [], []
                for half in range(2):
                    sink = sink_ref[2 * pair + half] * LOG2E
                    s = s_both[:, half * 3 * blk:(half + 1) * 3 * blk]
                    s_l = jnp.where(left_ok, s[:, :blk], NEG_BIG)
                    s_m = s[:, blk:2 * blk]
                    s_r = jnp.where(right_ok, s[:, 2 * blk:], NEG_BIG)
                    m = jnp.max(jnp.maximum(jnp.maximum(s_l, s_r), s_m), axis=-1, keepdims=True)
                    m = jnp.maximum(m, sink)
                    probs += [jnp.exp2(s_l - m), jnp.exp2(s_m - m), jnp.exp2(s_r - m)]
                    shifts.append(sink - m)
                acc = _dot(jnp.concatenate(probs, axis=1).astype(BF16), v_aug)
                den = acc[:, V7X_LANES:] + jnp.exp2(jnp.where(lo, shifts[0], shifts[1]))
                outs.append(acc[:, :V7X_LANES] / den)
        o_ref[pl.ds(r0, blk), :] = jnp.concatenate(outs, axis=1).astype(BF16)
        return carry

    lax.fori_loop(0, nb, body, 0, unroll=4)


def _attention(q, k4, v4, sink, batch, seq):
    t = q.shape[0]
    by_batch = lambda w: pl.BlockSpec((seq, w), lambda b: (b, 0))
    return pl.pallas_call(
        functools.partial(_attn_kernel, seq=seq),
        out_shape=jax.ShapeDtypeStruct((t, ATT_WIDTH), BF16),
        grid=(batch,),
        in_specs=[pl.BlockSpec(memory_space=pltpu.SMEM), by_batch(ATT_WIDTH),
                  by_batch(4 * V7X_LANES), by_batch(4 * V7X_LANES)],
        out_specs=by_batch(ATT_WIDTH),
        compiler_params=_params(1),
        name="window_attn",
    )(sink, q, k4, v4)


def _scan(x, axis, reverse, op, identity):
    n = x.shape[axis]
    idx = lax.broadcasted_iota(jnp.int32, x.shape, axis)
    d = 1
    while d < n:
        if reverse:
            x = op(x, jnp.where(idx < n - d, pltpu.roll(x, n - d, axis), identity))
        else:
            x = op(x, jnp.where(idx >= d, pltpu.roll(x, d, axis), identity))
        d *= 2
    return x


def _mlstm_kernel(qraw_ref, kraw_ref, v_ref, og_ref, gr_ref, cw_ref, cb_ref, ng_ref,
                  y_ref,
                  qs_ref, ks_ref, b_ref, dm_ref, w_ref, r_ref, am_ref, bt_ref,
                  cst_ref, mst_ref, cin_ref, minf_ref, minb_ref, hs_ref, *, seq):
    L = MLSTM_CHUNK
    H = MLSTM_HEADS
    dh = MLSTM_HEAD_DIM
    nc = seq // L
    width = MLSTM_WIDTH
    halo = 16

    def head(h):
        return slice(h * dh, (h + 1) * dh)

    def with_ones(v):
        return jnp.concatenate([v, jnp.ones_like(v)], axis=1)

    assert nc * 8 == V7X_LANES and L == V7X_LANES
    shape = (nc * 8, L)
    gi = gr_ref[:, 0:8, :].reshape(shape)
    lf = _log_sigmoid(gr_ref[:, 8:16, :].reshape(shape))
    fwd = (lax.broadcasted_iota(jnp.int32, shape, 0) & 7) < H
    lane = lax.broadcasted_iota(jnp.int32, shape, 1)
    b = jnp.where(fwd, _scan(lf, 1, False, jnp.add, 0.0), _scan(lf, 1, True, jnp.add, 0.0))
    b_tot = jnp.sum(jnp.where(lane == jnp.where(fwd, L - 1, 0), b, 0.0), axis=1, keepdims=True)
    a = b_tot - b + gi
    a_max = jnp.max(a, axis=1, keepdims=True)
    r = gi - b
    d_max = b + jnp.where(fwd, _scan(r, 1, False, jnp.maximum, NEG_BIG),
                          _scan(r, 1, True, jnp.maximum, NEG_BIG))
    r_ref[...] = (r * LOG2E).reshape(nc, 8, L)
    cols = ((b * LOG2E).T, (d_max * LOG2E).T, jnp.exp(a - a_max).T)
    stats = (jnp.broadcast_to(a_max, shape).T[0:1, :], jnp.broadcast_to(b_tot, shape).T[0:1, :])
    for c in range(nc):
        shift = (V7X_LANES - 8 * c) % V7X_LANES
        for src, dst_ref in zip(cols, (b_ref, dm_ref, w_ref)):
            dst_ref[c * L:(c + 1) * L, :] = pltpu.roll(src, shift, 1) if shift else src
        for src, dst_ref in zip(stats, (am_ref, bt_ref)):
            dst_ref[c] = pltpu.roll(src, shift, 1) if shift else src

    def prep(c, carry):
        r0 = pl.multiple_of(c * L, L)
        rows = pl.ds(r0, L)
        rid = lax.broadcasted_iota(jnp.int32, (L, width), 0)
        has_prev = jnp.where(c > 0, 1.0, 0.0)
        has_next = jnp.where(c < nc - 1, 1.0, 0.0)
        prev_at = pl.multiple_of(jnp.maximum(r0 - halo, 0), halo)
        next_at = pl.multiple_of(jnp.minimum(r0 + L, seq - halo), halo)
        for idx, (raw_ref, dst_ref) in enumerate(((qraw_ref, qs_ref), (kraw_ref, ks_ref))):
            cols = slice(idx * width, (idx + 1) * width)
            cur = raw_ref[rows, :].astype(F32)
            prev_row = raw_ref[pl.ds(prev_at, halo), :].astype(F32)[halo - 1:halo, :] * has_prev
            next_row = raw_ref[pl.ds(next_at, halo), :].astype(F32)[0:1, :] * has_next
            before = jnp.where(rid == 0, prev_row, pltpu.roll(cur, 1, 0))
            after = jnp.where(rid == L - 1, next_row, pltpu.roll(cur, L - 1, 0))
            u = (before * cw_ref[0:1, cols] + cur * cw_ref[1:2, cols] + after * cw_ref[2:3, cols]
                 + cb_ref[:, cols])
            u = u / (1.0 + jnp.exp2(u * (-LOG2E)))
            if idx == 1:
                u = u * (dh ** -0.5)
            dst_ref[rows, :] = u.astype(BF16)
        return carry

    lax.fori_loop(0, nc, prep, 0)

    cst_ref[...] = jnp.zeros_like(cst_ref)
    mst_ref[...] = jnp.zeros_like(mst_ref)

    def scan(i, carry):
        c_f = i
        c_b = nc - 1 - i
        lane1 = lax.broadcasted_iota(jnp.int32, (1, V7X_LANES), 1)
        fwd = lane1 < H
        bt = jnp.where(fwd, bt_ref[c_f], bt_ref[c_b])
        am = jnp.where(fwd, am_ref[c_f], am_ref[c_b])
        m_old = mst_ref[...]
        m_new = jnp.maximum(bt + m_old, am)
        s_prev = jnp.exp(bt + m_old - m_new)
        s_loc = jnp.exp(am - m_new)
        minf_ref[c_f] = m_old
        minb_ref[c_b] = m_old
        mst_ref[...] = m_new
        for d, c in enumerate((c_f, c_b)):
            rows = pl.ds(pl.multiple_of(c * L, L), L)
            wc = w_ref[rows, :]
            for h in range(H):
                j = d * H + h
                kw = ks_ref[rows, head(h)].astype(F32) * wc[:, j:j + 1]
                c_loc = _dot_tn(kw.astype(BF16), with_ones(v_ref[rows, head(h)]))
                c_old = cst_ref[j]
                cin_ref[c, j] = c_old.astype(BF16)
                cst_ref[j] = s_prev[:, j:j + 1] * c_old + s_loc[:, j:j + 1] * c_loc
        return carry

    lax.fori_loop(0, nc, scan, 0)

    def emit(c):
        rows = pl.ds(pl.multiple_of(c * L, L), L)
        bc = b_ref[rows, :]
        rc = r_ref[c]
        lane1 = lax.broadcasted_iota(jnp.int32, (1, V7X_LANES), 1)
        m_in = jnp.where(lane1 < H, minf_ref[c], minb_ref[c]) * LOG2E
        m_t = jnp.maximum(bc + m_in, dm_ref[rows, :])
        b_rel = bc - m_t
        t_id = lax.broadcasted_iota(jnp.int32, (L, L), 0)
        s_id = lax.broadcasted_iota(jnp.int32, (L, L), 1)
        masks = (s_id <= t_id, s_id >= t_id)
        for h in range(H):
            q = qs_ref[rows, head(h)]
            qf = q.astype(F32)
            v1 = with_ones(v_ref[rows, head(h)])
            s = _dot_nt(q, ks_ref[rows, head(h)])
            hsum = None
            for d in range(2):
                j = d * H + h
                bcol = jnp.broadcast_to(b_rel[:, j:j + 1], (L, L))
                mcol = jnp.broadcast_to(m_t[:, j:j + 1], (L, L))
                p = jnp.exp2(jnp.where(masks[d], bcol + rc[j:j + 1, :], NEG_BIG))
                scale_in = jnp.exp2(bcol + m_in[:, j:j + 1])
                lhs = jnp.concatenate([(s * p).astype(BF16), (scale_in * qf).astype(BF16)], axis=1)
                tot = _dot(lhs, jnp.concatenate([v1, cin_ref[c, j]], axis=0))
                hd = tot[:, :dh] / jnp.maximum(jnp.abs(tot[:, dh:]), jnp.exp2(-mcol))
                hsum = hd if hsum is None else hsum + hd
            hs_ref[rows, head(h)] = hsum

    def finish(c):
        rows = pl.ds(pl.multiple_of(c * L, L), L)
        for h in range(H):
            hsum = hs_ref[rows, head(h)]
            xc = hsum - jnp.mean(hsum, axis=-1, keepdims=True)
            var = jnp.mean(xc * xc, axis=-1, keepdims=True)
            y = xc * lax.rsqrt(var + NORM_EPS) * ng_ref[:, head(h)] * og_ref[rows, head(h)].astype(F32)
            y_ref[rows, head(h)] = y.astype(BF16)

    def emit_and_finish(c, carry):
        finish(c - 1)
        emit(c)
        return carry

    emit(0)
    lax.fori_loop(1, nc, emit_and_finish, 0)
    finish(nc - 1)


def _mlstm(qraw, kraw, v, og, grow, conv_w, conv_b, norm_g, batch, seq):
    t = qraw.shape[0]
    nc = seq // MLSTM_CHUNK
    nd = 2 * MLSTM_HEADS
    by_batch = lambda w: pl.BlockSpec((seq, w), lambda b: (b, 0))
    scratch = [
        pltpu.VMEM((seq, MLSTM_WIDTH), BF16),
        pltpu.VMEM((seq, MLSTM_WIDTH), BF16),
        pltpu.VMEM((seq, V7X_LANES), F32),
        pltpu.VMEM((seq, V7X_LANES), F32),
        pltpu.VMEM((seq, V7X_LANES), F32),
        pltpu.VMEM((nc, 8, MLSTM_CHUNK), F32),
        pltpu.VMEM((nc, 1, V7X_LANES), F32),
        pltpu.VMEM((nc, 1, V7X_LANES), F32),
        pltpu.VMEM((nd, MLSTM_HEAD_DIM, 2 * MLSTM_HEAD_DIM), F32),
        pltpu.VMEM((1, V7X_LANES), F32),
        pltpu.VMEM((nc, nd, MLSTM_HEAD_DIM, 2 * MLSTM_HEAD_DIM), BF16),
        pltpu.VMEM((nc, 1, V7X_LANES), F32),
        pltpu.VMEM((nc, 1, V7X_LANES), F32),
        pltpu.VMEM((seq, MLSTM_WIDTH), F32),
    ]
    return pl.pallas_call(
        functools.partial(_mlstm_kernel, seq=seq),
        out_shape=jax.ShapeDtypeStruct((t, MLSTM_WIDTH), BF16),
        grid=(batch,),
        in_specs=[by_batch(MLSTM_WIDTH)] * 4 + [
            pl.BlockSpec((nc, 16, MLSTM_CHUNK), lambda b: (b, 0, 0)),
            _resident(conv_w.shape), _resident(conv_b.shape), _resident(norm_g.shape)],
        out_specs=by_batch(MLSTM_WIDTH),
        scratch_shapes=scratch,
        compiler_params=_params(1),
        name="bidir_mlstm",
    )(qraw, kraw, v, og, grow, conv_w, conv_b, norm_g)


def _merge_kernel(x_ref, ya_ref, ym_ref, ga_ref, gm_ref, wa_ref, wb_ref, wo_ref, o_ref):
    merged = (ga_ref[...].astype(F32) * _dot(ya_ref[...], wa_ref[...])
              + gm_ref[...].astype(F32) * _dot(ym_ref[...], wb_ref[...]))
    o_ref[...] = x_ref[...] + _dot(merged.astype(BF16), wo_ref[...])


def _merge(x, ya, ym, ga, gm, wa, wb, wo):
    t, d = x.shape
    tm = MERGE_ROWS
    return pl.pallas_call(
        _merge_kernel,
        out_shape=jax.ShapeDtypeStruct((t, d), F32),
        grid=(t // tm,),
        in_specs=[_rows(tm, d), _rows(tm, ATT_WIDTH), _rows(tm, MLSTM_WIDTH), _rows(tm, d), _rows(tm, d),
                  _resident(wa.shape), _resident(wb.shape), _resident(wo.shape)],
        out_specs=_rows(tm, d),
        compiler_params=_params(1),
        name="merge_out",
    )(x, ya, ym, ga, gm, wa, wb, wo)


def _proj_weights(w_in, gate_bias, q_gain, k_gain):
    d = w_in.shape[0]
    H = MLSTM_HEADS
    o_q = ATT_WIDTH
    o_k = o_q + ATT_KV_WIDTH
    o_v = o_k + ATT_KV_WIDTH
    o_qm = o_v + MLSTM_WIDTH
    o_km = o_qm + MLSTM_WIDTH
    o_vm = o_km + MLSTM_WIDTH
    o_om = o_vm + MLSTM_WIDTH
    o_g = o_om + 4 * H
    wg = w_in[:, o_om:o_g]
    order_i = jnp.concatenate([wg[:, 0:H], wg[:, 2 * H:3 * H]], axis=1)
    order_f = jnp.concatenate([wg[:, H:2 * H], wg[:, 3 * H:4 * H]], axis=1)
    gate_row = jnp.concatenate([order_i, order_f], axis=1).T
    b_i = jnp.concatenate([gate_bias[0:H], gate_bias[2 * H:3 * H]])
    b_f = jnp.concatenate([gate_bias[H:2 * H], gate_bias[3 * H:4 * H]])
    return {
        "qkv": w_in[:, :o_v].astype(BF16),
        "qkm": w_in[:, o_v:o_km].astype(BF16),
        "vo": w_in[:, o_km:o_om].astype(BF16),
        "gate_row": gate_row.astype(BF16),
        "bias_row": jnp.concatenate([b_i, b_f]).reshape(4 * H, 1).astype(F32),
        "merge": w_in[:, o_g:].astype(BF16),
        "q_gain": jnp.tile(q_gain, ATT_HEADS).reshape(1, ATT_WIDTH).astype(F32),
        "k_gain": jnp.tile(k_gain, ATT_KV_HEADS).reshape(1, ATT_KV_WIDTH).astype(F32),
    }


def kernel(x, positions, ffn1_norm, ffn1_w_gate, ffn1_w_up, ffn1_w_down, mix_norm, w_in, mlstm_gate_bias, attn_q_norm, attn_k_norm, attn_sink, mlstm_conv_w, mlstm_conv_b, mlstm_out_norm, w_branch_attn, w_branch_mlstm, w_out, ffn2_norm, ffn2_w_gate, ffn2_w_up, ffn2_w_down, block_out_norm):
    batch, seq, d = x.shape
    depth = w_in.shape[0]
    t = batch * seq
    xt = x.reshape(t, d)
    cos_t, sin_t = _rope_tables(positions)
    for l in range(depth):
        xt = _ffn(xt, ffn1_norm[l], ffn1_w_gate[l].astype(BF16), ffn1_w_up[l].astype(BF16),
                  ffn1_w_down[l].astype(BF16))
        pw = _proj_weights(w_in[l], mlstm_gate_bias[l], attn_q_norm[l], attn_k_norm[l])
        qa, k4, v4, qm, km, vm, og, grow, ga, gm = _proj(xt, mix_norm[l], pw, cos_t, sin_t)
        ya = _attention(qa, k4, v4, attn_sink[l].astype(F32), batch, seq)
        ym = _mlstm(qm, km, vm, og, grow, mlstm_conv_w[l].astype(F32),
                    mlstm_conv_b[l].reshape(1, -1).astype(F32),
                    mlstm_out_norm[l].reshape(1, -1).astype(F32), batch, seq)
        xt = _merge(xt, ya, ym, ga, gm, w_branch_attn[l].astype(BF16), w_branch_mlstm[l].astype(BF16),
                    w_out[l].astype(BF16))
        xt = _ffn(xt, ffn2_norm[l], ffn2_w_gate[l].astype(BF16), ffn2_w_up[l].astype(BF16),
                  ffn2_w_down[l].astype(BF16), final_gain=block_out_norm[l])
    return xt.reshape(batch, seq, d)
```

```python
import functools

import numpy as np
import jax
import jax.numpy as jnp
from jax import lax
from jax.experimental import pallas as pl
from jax.experimental.pallas import tpu as pltpu

F32 = jnp.float32
BF16 = jnp.bfloat16

ATT_HEAD_DIM = 64
ATT_HEADS = 8
ATT_KV_HEADS = 2
ATT_WIDTH = ATT_HEADS * ATT_HEAD_DIM
ATT_KV_WIDTH = ATT_KV_HEADS * ATT_HEAD_DIM
WINDOW = 128
ATT_BLOCK = 128
ROPE_DIM = ATT_HEAD_DIM // 4
ROPE_THETA = 500000.0
MLSTM_HEADS = 4
MLSTM_HEAD_DIM = 128
MLSTM_WIDTH = MLSTM_HEADS * MLSTM_HEAD_DIM
MLSTM_CHUNK = 128
NORM_EPS = 1e-6
NEG_BIG = -1e30
LOG2E = 1.4426950408889634

V7X_LANES = 128
V7X_MXU_COLS = 256
V7X_VMEM_LIMIT_BYTES = 56 * 1024 * 1024

FFN_ROWS = 1024
PROJ_ROWS = 512
MERGE_ROWS = 1024
ROPE_ROWS = 2048


def _params(n_axes):
    return pltpu.CompilerParams(
        dimension_semantics=("parallel",) * n_axes,
        vmem_limit_bytes=V7X_VMEM_LIMIT_BYTES,
    )


def _resident(shape):
    nd = len(shape)
    return pl.BlockSpec(shape, lambda *_: (0,) * nd, pipeline_mode=pl.Buffered(1))


def _rows(tm, width):
    return pl.BlockSpec((tm, width), lambda i: (i, 0))


def _rms(x, gain):
    ms = jnp.mean(x * x, axis=-1, keepdims=True)
    return x * lax.rsqrt(ms + NORM_EPS) * gain


def _dot(a, b):
    return jnp.dot(a, b, preferred_element_type=F32)


def _dot_nt(a, b):
    return lax.dot_general(a, b, (((1,), (1,)), ((), ())), preferred_element_type=F32)


def _dot_tn(a, b):
    return lax.dot_general(a, b, (((0,), (0,)), ((), ())), preferred_element_type=F32)


def _log_sigmoid(x):
    return jnp.minimum(x, 0.0) - jnp.log1p(jnp.exp(-jnp.abs(x)))


def _rope_kernel(pos_ref, freq_ref, cos_ref, sin_ref):
    ang = pos_ref[...].astype(F32) * freq_ref[...]
    j = lax.broadcasted_iota(jnp.int32, ang.shape, 1) % ATT_HEAD_DIM
    half = ROPE_DIM // 2
    c = jnp.cos(ang)
    s = jnp.sin(ang)
    cos_ref[...] = jnp.where(j < ROPE_DIM, c, 1.0)
    sin_ref[...] = jnp.where(j < half, -s, jnp.where(j < ROPE_DIM, s, 0.0))


def _rope_tables(positions):
    t = positions.size
    half = ROPE_DIM // 2
    inv_freq = np.power(np.float32(ROPE_THETA),
                        -np.arange(half, dtype=np.float32) * np.float32(2.0 / ROPE_DIM)).astype(np.float32)
    lane = np.arange(V7X_LANES) % ATT_HEAD_DIM
    freq = np.where(lane < ROPE_DIM, inv_freq[lane % half], 0.0).astype(np.float32)[None, :]
    pos = positions.reshape(t, 1)
    return pl.pallas_call(
        _rope_kernel,
        out_shape=[jax.ShapeDtypeStruct((t, V7X_LANES), F32)] * 2,
        grid=(t // ROPE_ROWS,),
        in_specs=[_rows(ROPE_ROWS, 1), _resident((1, V7X_LANES))],
        out_specs=[_rows(ROPE_ROWS, V7X_LANES)] * 2,
        compiler_params=_params(1),
        name="rope_tables",
    )(pos, jnp.asarray(freq))


def _ffn_kernel(*refs, d_ff, final_norm):
    if final_norm:
        x_ref, g_ref, wg_ref, wu_ref, wd_ref, go_ref, o_ref, hn_ref, a_ref = refs
    else:
        x_ref, g_ref, wg_ref, wu_ref, wd_ref, o_ref, hn_ref, a_ref = refs
    hn_ref[...] = _rms(x_ref[...], g_ref[...]).astype(BF16)
    for c0 in range(0, d_ff, V7X_MXU_COLS):
        cols = slice(c0, c0 + V7X_MXU_COLS)
        hn = hn_ref[...]
        gate = _dot(hn, wg_ref[:, cols])
        up = _dot(hn, wu_ref[:, cols])
        a_ref[:, cols] = (gate * jax.nn.sigmoid(gate) * up).astype(BF16)
    out = x_ref[...] + 0.5 * _dot(a_ref[...], wd_ref[...])
    if final_norm:
        out = _rms(out, go_ref[...])
    o_ref[...] = out


def _ffn(x, gain, wg, wu, wd, final_gain=None):
    t, d = x.shape
    d_ff = wg.shape[1]
    assert d_ff % V7X_MXU_COLS == 0 and t % FFN_ROWS == 0
    final_norm = final_gain is not None
    args = [x, gain.reshape(1, d), wg, wu, wd]
    in_specs = [_rows(FFN_ROWS, d), _resident((1, d)), _resident((d, d_ff)),
                _resident((d, d_ff)), _resident((d_ff, d))]
    if final_norm:
        args.append(final_gain.reshape(1, d))
        in_specs.append(_resident((1, d)))
    return pl.pallas_call(
        functools.partial(_ffn_kernel, d_ff=d_ff, final_norm=final_norm),
        out_shape=jax.ShapeDtypeStruct((t, d), F32),
        grid=(t // FFN_ROWS,),
        in_specs=in_specs,
        out_specs=_rows(FFN_ROWS, d),
        scratch_shapes=[pltpu.VMEM((FFN_ROWS, d), BF16), pltpu.VMEM((FFN_ROWS, d_ff), BF16)],
        compiler_params=_params(1),
        name="ffn_final" if final_norm else "ffn",
    )(*args)


def _head_rms(x, ones_bd, gain):
    ss = _dot((x * x).astype(BF16), ones_bd)
    return x * lax.rsqrt(ss * (1.0 / ATT_HEAD_DIM) + NORM_EPS) * gain


def _rope(x, cos, sin):
    n = x.shape[1]
    half = ROPE_DIM // 2
    j = lax.broadcasted_iota(jnp.int32, x.shape, 1) % ATT_HEAD_DIM
    partner = jnp.where(j < half, pltpu.roll(x, n - half, 1), pltpu.roll(x, half, 1))
    return x * cos + partner * sin


def _lo_hi(x):
    lo = lax.broadcasted_iota(jnp.int32, x.shape, 1) < ATT_HEAD_DIM
    xr = pltpu.roll(x, ATT_HEAD_DIM, 1)
    zero = jnp.zeros_like(x)
    return jnp.concatenate([jnp.where(lo, x, zero), jnp.where(lo, zero, xr),
                            jnp.where(lo, xr, zero), jnp.where(lo, zero, x)], axis=1)


def _proj_kernel(x_ref, g_ref, wqkv_ref, wqkm_ref, wvo_ref, wgr_ref, bgr_ref,
                 wmg_ref, qg_ref, kg_ref, cos_ref, sin_ref, ones_ref,
                 qa_ref, k4_ref, v4_ref, qm_ref, km_ref, vm_ref, og_ref, gr_ref,
                 ga_ref, gmm_ref, hn_ref):
    hn_ref[...] = _rms(x_ref[...], g_ref[...]).astype(BF16)
    hn = hn_ref[...]
    cos = cos_ref[...]
    sin = sin_ref[...]
    ones_bd = ones_ref[...]

    q = _dot(hn, wqkv_ref[:, :ATT_WIDTH])
    k = _dot(hn, wqkv_ref[:, ATT_WIDTH:ATT_WIDTH + ATT_KV_WIDTH])
    v = _dot(hn, wqkv_ref[:, ATT_WIDTH + ATT_KV_WIDTH:])
    qm_ref[...] = _dot(hn, wqkm_ref[:, :MLSTM_WIDTH]).astype(BF16)
    q = _head_rms(q, ones_bd, qg_ref[...])
    km_ref[...] = _dot(hn, wqkm_ref[:, MLSTM_WIDTH:]).astype(BF16)
    k = _head_rms(k, ones_bd[:ATT_KV_WIDTH, :ATT_KV_WIDTH], kg_ref[...])
    vm_ref[...] = _dot(hn, wvo_ref[:, :MLSTM_WIDTH]).astype(BF16)
    og_ref[...] = jax.nn.sigmoid(_dot(hn, wvo_ref[:, MLSTM_WIDTH:])).astype(BF16)
    gr = _dot_nt(wgr_ref[...], hn) + bgr_ref[...]
    for c in range(gr_ref.shape[0]):
        gr_ref[c] = gr[:, c * MLSTM_CHUNK:(c + 1) * MLSTM_CHUNK]

    reps = ATT_WIDTH // V7X_LANES
    q = _rope(q, jnp.concatenate([cos] * reps, axis=1), jnp.concatenate([sin] * reps, axis=1))
    qa_ref[...] = (q * (ATT_HEAD_DIM ** -0.5 * LOG2E)).astype(BF16)
    k4_ref[...] = _lo_hi(_rope(k, cos, sin)).astype(BF16)
    v4_ref[...] = _lo_hi(v).astype(BF16)

    d = ga_ref.shape[1]
    piece = 2 * V7X_MXU_COLS
    for c0 in range(0, d, piece):
        ga_ref[:, c0:c0 + piece] = jax.nn.sigmoid(_dot(hn, wmg_ref[:, c0:c0 + piece])).astype(BF16)
    for c0 in range(0, d, piece):
        gmm_ref[:, c0:c0 + piece] = jax.nn.sigmoid(_dot(hn, wmg_ref[:, d + c0:d + c0 + piece])).astype(BF16)


def _proj(x, gain, w, cos_t, sin_t):
    t, d = x.shape
    tm = PROJ_ROWS
    nck = tm // MLSTM_CHUNK
    heads = np.arange(ATT_WIDTH) // ATT_HEAD_DIM
    ones_bd = jnp.asarray((heads[:, None] == heads[None, :]).astype(np.float32), dtype=BF16)
    out_shape = [
        jax.ShapeDtypeStruct((t, ATT_WIDTH), BF16),
        jax.ShapeDtypeStruct((t, 4 * V7X_LANES), BF16),
        jax.ShapeDtypeStruct((t, 4 * V7X_LANES), BF16),
        jax.ShapeDtypeStruct((t, MLSTM_WIDTH), BF16),
        jax.ShapeDtypeStruct((t, MLSTM_WIDTH), BF16),
        jax.ShapeDtypeStruct((t, MLSTM_WIDTH), BF16),
        jax.ShapeDtypeStruct((t, MLSTM_WIDTH), BF16),
        jax.ShapeDtypeStruct((t // MLSTM_CHUNK, 16, MLSTM_CHUNK), F32),
        jax.ShapeDtypeStruct((t, d), BF16),
        jax.ShapeDtypeStruct((t, d), BF16),
    ]
    out_specs = [
        _rows(tm, ATT_WIDTH), _rows(tm, 4 * V7X_LANES), _rows(tm, 4 * V7X_LANES),
        _rows(tm, MLSTM_WIDTH), _rows(tm, MLSTM_WIDTH), _rows(tm, MLSTM_WIDTH), _rows(tm, MLSTM_WIDTH),
        pl.BlockSpec((nck, 16, MLSTM_CHUNK), lambda i: (i, 0, 0)),
        _rows(tm, d), _rows(tm, d),
    ]
    args = [x, gain.reshape(1, d), w["qkv"], w["qkm"], w["vo"], w["gate_row"],
            w["bias_row"], w["merge"], w["q_gain"], w["k_gain"], cos_t, sin_t, ones_bd]
    in_specs = [_rows(tm, d)] + [_resident(a.shape) for a in args[1:10]] + [
        _rows(tm, V7X_LANES), _rows(tm, V7X_LANES), _resident(ones_bd.shape)]
    return pl.pallas_call(
        _proj_kernel,
        out_shape=out_shape,
        grid=(t // tm,),
        in_specs=in_specs,
        out_specs=out_specs,
        scratch_shapes=[pltpu.VMEM((tm, d), BF16)],
        compiler_params=_params(1),
        name="mixer_proj",
    )(*args)


def _attn_kernel(sink_ref, q_ref, k4_ref, v4_ref, o_ref, *, seq):
    blk = ATT_BLOCK
    assert WINDOW == blk
    nb = seq // blk
    group = ATT_HEADS // ATT_KV_HEADS

    def body(n, carry):
        r0 = pl.multiple_of(n * blk, blk)
        left = pl.multiple_of(jnp.maximum(r0 - blk, 0), blk)
        right = pl.multiple_of(jnp.minimum(r0 + blk, seq - blk), blk)
        qb = q_ref[pl.ds(r0, blk), :]
        kb = jnp.concatenate([k4_ref[pl.ds(left, blk), :], k4_ref[pl.ds(r0, blk), :],
                              k4_ref[pl.ds(right, blk), :]], axis=0)
        vb = jnp.concatenate([v4_ref[pl.ds(left, blk), :], v4_ref[pl.ds(r0, blk), :],
                              v4_ref[pl.ds(right, blk), :]], axis=0)
        ql = lax.broadcasted_iota(jnp.int32, (blk, blk), 0)
        kl = lax.broadcasted_iota(jnp.int32, (blk, blk), 1)
        left_ok = kl >= ql + jnp.where(n > 0, 0, blk)
        right_ok = kl + jnp.where(n < nb - 1, 0, blk) <= ql
        lo = lax.broadcasted_iota(jnp.int32, (blk, V7X_LANES), 1) < ATT_HEAD_DIM
        lo_band = lax.broadcasted_iota(jnp.int32, (3 * blk, V7X_LANES), 1) < ATT_HEAD_DIM
        ones_lo = jnp.where(lo_band, 1.0, 0.0).astype(BF16)
        ones_hi = jnp.where(lo_band, 0.0, 1.0).astype(BF16)
        outs = []
        for g in range(ATT_KV_HEADS):
            kv_lo = slice(2 * g * V7X_LANES, (2 * g + 1) * V7X_LANES)
            kv_hi = slice((2 * g + 1) * V7X_LANES, (2 * g + 2) * V7X_LANES)
            v_aug = jnp.concatenate([jnp.concatenate([vb[:, kv_lo], ones_lo], axis=1),
                                     jnp.concatenate([vb[:, kv_hi], ones_hi], axis=1)], axis=0)
            k_both = jnp.concatenate([kb[:, kv_lo], kb[:, kv_hi]], axis=0)
            for p in range(group // 2):
                pair = g * (group // 2) + p
                qp = qb[:, pair * V7X_LANES:(pair + 1) * V7X_LANES]
                s_both = _dot_nt(qp, k_both)
                probs, shifts = [], []
                for half in range(2):
                    sink = sink_ref[2 * pair + half] * LOG2E
                    s = s_both[:, half * 3 * blk:(half + 1) * 3 * blk]
                    s_l = jnp.where(left_ok, s[:, :blk], NEG_BIG)
                    s_m = s[:, blk:2 * blk]
                    s_r = jnp.where(right_ok, s[:, 2 * blk:], NEG_BIG)
                    m = jnp.max(jnp.maximum(jnp.maximum(s_l, s_r), s_m), axis=-1, keepdims=True)
                    m = jnp.maximum(m, sink)
                    probs += [jnp.exp2(s_l - m), jnp.exp2(s_m - m), jnp.exp2(s_r - m)]
                    shifts.append(sink - m)
                acc = _dot(jnp.concatenate(probs, axis=1).astype(BF16), v_aug)
                den = acc[:, V7X_LANES:] + jnp.exp2(jnp.where(lo, shifts[0], shifts[1]))
                outs.append(acc[:, :V7X_LANES] / den)
        o_ref[pl.ds(r0, blk), :] = jnp.concatenate(outs, axis=1).astype(BF16)
        return carry

    lax.fori_loop(0, nb, body, 0, unroll=8)


def _attention(q, k4, v4, sink, batch, seq):
    t = q.shape[0]
    by_batch = lambda w: pl.BlockSpec((seq, w), lambda b: (b, 0))
    return pl.pallas_call(
        functools.partial(_attn_kernel, seq=seq),
        out_shape=jax.ShapeDtypeStruct((t, ATT_WIDTH), BF16),
        grid=(batch,),
        in_specs=[pl.BlockSpec(memory_space=pltpu.SMEM), by_batch(ATT_WIDTH),
                  by_batch(4 * V7X_LANES), by_batch(4 * V7X_LANES)],
        out_specs=by_batch(ATT_WIDTH),
        compiler_params=_params(1),
        name="window_attn",
    )(sink, q, k4, v4)


def _scan(x, axis, reverse, op, identity):
    n = x.shape[axis]
    idx = lax.broadcasted_iota(jnp.int32, x.shape, axis)
    d = 1
    while d < n:
        if reverse:
            x = op(x, jnp.where(idx < n - d, pltpu.roll(x, n - d, axis), identity))
        else:
            x = op(x, jnp.where(idx >= d, pltpu.roll(x, d, axis), identity))
        d *= 2
    return x


def _mlstm_kernel(qraw_ref, kraw_ref, v_ref, og_ref, gr_ref, cw_ref, cb_ref, ng_ref,
                  y_ref,
                  ks_ref, qt_ref, kt_ref, vt_ref, rcol_ref, b_ref, dm_ref, w_ref, r_ref, am_ref, bt_ref,
                  cst_ref, mst_ref, cin_ref, minf_ref, minb_ref, hs_ref, *, seq):
    L = MLSTM_CHUNK
    H = MLSTM_HEADS
    dh = MLSTM_HEAD_DIM
    nc = seq // L
    width = MLSTM_WIDTH
    halo = 16
    row8 = lax.broadcasted_iota(jnp.int32, (8, L), 0)

    def head(h):
        return slice(h * dh, (h + 1) * dh)

    assert nc * 8 == V7X_LANES and L == V7X_LANES
    shape = (nc * 8, L)
    gi = gr_ref[:, 0:8, :].reshape(shape)
    lf = _log_sigmoid(gr_ref[:, 8:16, :].reshape(shape))
    fwd = (lax.broadcasted_iota(jnp.int32, shape, 0) & 7) < H
    lane = lax.broadcasted_iota(jnp.int32, shape, 1)
    b = jnp.where(fwd, _scan(lf, 1, False, jnp.add, 0.0), _scan(lf, 1, True, jnp.add, 0.0))
    b_tot = jnp.sum(jnp.where(lane == jnp.where(fwd, L - 1, 0), b, 0.0), axis=1, keepdims=True)
    a = b_tot - b + gi
    a_max = jnp.max(a, axis=1, keepdims=True)
    r = gi - b
    d_max = b + jnp.where(fwd, _scan(r, 1, False, jnp.maximum, NEG_BIG),
                          _scan(r, 1, True, jnp.maximum, NEG_BIG))
    r_ref[...] = (r * LOG2E).reshape(nc, 8, L)
    b_ref[...] = (b * LOG2E).reshape(nc, 8, L)
    dm_ref[...] = (d_max * LOG2E).reshape(nc, 8, L)
    w_ref[...] = jnp.exp(a - a_max).reshape(nc, 8, L)
    am_ref[...] = jnp.broadcast_to(a_max, shape).reshape(nc, 8, L)
    bt_ref[...] = jnp.broadcast_to(b_tot, shape).reshape(nc, 8, L)

    def prep(c, carry):
        r0 = pl.multiple_of(c * L, L)
        rows = pl.ds(r0, L)
        rid = lax.broadcasted_iota(jnp.int32, (L, width), 0)
        has_prev = jnp.where(c > 0, 1.0, 0.0)
        has_next = jnp.where(c < nc - 1, 1.0, 0.0)
        prev_at = pl.multiple_of(jnp.maximum(r0 - halo, 0), halo)
        next_at = pl.multiple_of(jnp.minimum(r0 + L, seq - halo), halo)
        acts = []
        for idx, raw_ref in enumerate((qraw_ref, kraw_ref)):
            cols = slice(idx * width, (idx + 1) * width)
            cur = raw_ref[rows, :].astype(F32)
            prev_row = raw_ref[pl.ds(prev_at, halo), :].astype(F32)[halo - 1:halo, :] * has_prev
            next_row = raw_ref[pl.ds(next_at, halo), :].astype(F32)[0:1, :] * has_next
            before = jnp.where(rid == 0, prev_row, pltpu.roll(cur, 1, 0))
            after = jnp.where(rid == L - 1, next_row, pltpu.roll(cur, L - 1, 0))
            u = (before * cw_ref[0:1, cols] + cur * cw_ref[1:2, cols] + after * cw_ref[2:3, cols]
                 + cb_ref[:, cols])
            u = u / (1.0 + jnp.exp2(u * (-LOG2E)))
            if idx == 1:
                u = u * (dh ** -0.5)
            acts.append(u)
        q_act, k_act = acts
        ks_ref[rows, :] = k_act.astype(BF16)
        ones_rows = jnp.ones((halo, L), BF16)
        for h in range(H):
            qt_ref[c, h] = q_act[:, head(h)].T.astype(BF16)
            kt_ref[c, h] = k_act[:, head(h)].T.astype(BF16)
            vt_ref[c, h, 0:dh, :] = v_ref[rows, head(h)].astype(F32).T.astype(BF16)
            vt_ref[c, h, dh:dh + halo, :] = ones_rows
        rcol_ref[rows, :] = jnp.concatenate([r_ref[c], jnp.zeros((L - 8, L), F32)], axis=0).T
        return carry

    lax.fori_loop(0, nc, prep, 0, unroll=2)

    cst_ref[...] = jnp.zeros_like(cst_ref)
    mst_ref[...] = jnp.zeros_like(mst_ref)

    def scan(i, carry):
        c_f = i
        c_b = nc - 1 - i
        fwd = row8 < H
        bt = jnp.where(fwd, bt_ref[c_f], bt_ref[c_b])
        am = jnp.where(fwd, am_ref[c_f], am_ref[c_b])
        m_old = mst_ref[...]
        m_new = jnp.maximum(bt + m_old, am)
        s_prev = jnp.exp(bt + m_old - m_new)
        s_loc = jnp.exp(am - m_new)
        minf_ref[c_f] = m_old
        minb_ref[c_b] = m_old
        mst_ref[...] = m_new
        for d, c in enumerate((c_f, c_b)):
            wr = w_ref[c]
            for h in range(H):
                j = d * H + h
                kw_t = (kt_ref[c, h].astype(F32) * wr[j:j + 1, :]).astype(BF16)
                c_loc = _dot_nt(vt_ref[c, h], kw_t)
                c_old = cst_ref[j]
                cin_ref[c, j] = c_old.astype(BF16)
                cst_ref[j] = s_prev[j:j + 1, :] * c_old + s_loc[j:j + 1, :] * c_loc
        return carry

    lax.fori_loop(0, nc, scan, 0, unroll=4)

    def emit(c):
        rows = pl.ds(pl.multiple_of(c * L, L), L)
        m_in = jnp.where(row8 < H, minf_ref[c], minb_ref[c]) * LOG2E
        bc = b_ref[c]
        m_t = jnp.maximum(bc + m_in, dm_ref[c])
        b_rel = bc - m_t
        scale_in = jnp.exp2(b_rel + m_in)
        floor = jnp.exp2(-m_t)
        rcol = rcol_ref[rows, :]
        s_id = lax.broadcasted_iota(jnp.int32, (L, L), 0)
        t_id = lax.broadcasted_iota(jnp.int32, (L, L), 1)
        masks = (s_id <= t_id, s_id >= t_id)
        for h in range(H):
            q_t = qt_ref[c, h]
            s_t = _dot(ks_ref[rows, head(h)], q_t)
            weights, sums, inters = [], [], []
            for d in range(2):
                j = d * H + h
                logw = jnp.broadcast_to(rcol[:, j:j + 1], (L, L)) + b_rel[j:j + 1, :]
                sc = s_t * jnp.exp2(jnp.where(masks[d], logw, NEG_BIG))
                sums.append(jnp.sum(sc, axis=0, keepdims=True))
                weights.append(sc.astype(BF16))
                inters.append(_dot(cin_ref[c, j], q_t))
            num = _dot(vt_ref[c, h, 0:dh, :], jnp.concatenate(weights, axis=1))
            hsum = None
            for d in range(2):
                j = d * H + h
                den = sums[d] + scale_in[j:j + 1, :] * inters[d][dh:dh + 1, :]
                inv = 1.0 / jnp.maximum(jnp.abs(den), floor[j:j + 1, :])
                hd = num[:, d * L:(d + 1) * L] * inv + inters[d][0:dh, :] * (scale_in[j:j + 1, :] * inv)
                hsum = hd if hsum is None else hsum + hd
            hs_ref[c, h] = hsum

    def finish(c):
        rows = pl.ds(pl.multiple_of(c * L, L), L)
        for h in range(H):
            hsum = hs_ref[c, h]
            xc = hsum - jnp.mean(hsum, axis=0, keepdims=True)
            var = jnp.mean(xc * xc, axis=0, keepdims=True)
            y = (xc * lax.rsqrt(var + NORM_EPS)).T * ng_ref[:, head(h)] * og_ref[rows, head(h)].astype(F32)
            y_ref[rows, head(h)] = y.astype(BF16)

    def emit_and_finish(c, carry):
        finish(c - 1)
        emit(c)
        return carry

    emit(0)
    lax.fori_loop(1, nc, emit_and_finish, 0, unroll=3)
    finish(nc - 1)


def _mlstm(qraw, kraw, v, og, grow, conv_w, conv_b, norm_g, batch, seq):
    t = qraw.shape[0]
    nc = seq // MLSTM_CHUNK
    nd = 2 * MLSTM_HEADS
    by_batch = lambda w: pl.BlockSpec((seq, w), lambda b: (b, 0))
    dh = MLSTM_HEAD_DIM
    aug = dh + 16
    tile = (nc, MLSTM_HEADS, dh, MLSTM_CHUNK)
    rowvec = pltpu.VMEM((nc, 8, MLSTM_CHUNK), F32)
    scratch = [
        pltpu.VMEM((seq, MLSTM_WIDTH), BF16),
        pltpu.VMEM(tile, BF16),
        pltpu.VMEM(tile, BF16),
        pltpu.VMEM((nc, MLSTM_HEADS, aug, MLSTM_CHUNK), BF16),
        pltpu.VMEM((seq, V7X_LANES), F32),
        rowvec,
        rowvec,
        rowvec,
        rowvec,
        rowvec,
        rowvec,
        pltpu.VMEM((nd, aug, dh), F32),
        pltpu.VMEM((8, MLSTM_CHUNK), F32),
        pltpu.VMEM((nc, nd, aug, dh), BF16),
        rowvec,
        rowvec,
        pltpu.VMEM(tile, F32),
    ]
    return pl.pallas_call(
        functools.partial(_mlstm_kernel, seq=seq),
        out_shape=jax.ShapeDtypeStruct((t, MLSTM_WIDTH), BF16),
        grid=(batch,),
        in_specs=[by_batch(MLSTM_WIDTH)] * 4 + [
            pl.BlockSpec((nc, 16, MLSTM_CHUNK), lambda b: (b, 0, 0)),
            _resident(conv_w.shape), _resident(conv_b.shape), _resident(norm_g.shape)],
        out_specs=by_batch(MLSTM_WIDTH),
        scratch_shapes=scratch,
        compiler_params=_params(1),
        name="bidir_mlstm",
    )(qraw, kraw, v, og, grow, conv_w, conv_b, norm_g)


def _merge_kernel(x_ref, ya_ref, ym_ref, ga_ref, gm_ref, wa_ref, wb_ref, wo_ref, o_ref):
    merged = (ga_ref[...].astype(F32) * _dot(ya_ref[...], wa_ref[...])
              + gm_ref[...].astype(F32) * _dot(ym_ref[...], wb_ref[...]))
    o_ref[...] = x_ref[...] + _dot(merged.astype(BF16), wo_ref[...])


def _merge(x, ya, ym, ga, gm, wa, wb, wo):
    t, d = x.shape
    tm = MERGE_ROWS
    return pl.pallas_call(
        _merge_kernel,
        out_shape=jax.ShapeDtypeStruct((t, d), F32),
        grid=(t // tm,),
        in_specs=[_rows(tm, d), _rows(tm, ATT_WIDTH), _rows(tm, MLSTM_WIDTH), _rows(tm, d), _rows(tm, d),
                  _resident(wa.shape), _resident(wb.shape), _resident(wo.shape)],
        out_specs=_rows(tm, d),
        compiler_params=_params(1),
        name="merge_out",
    )(x, ya, ym, ga, gm, wa, wb, wo)


def _proj_weights(w_in, gate_bias, q_gain, k_gain):
    d = w_in.shape[0]
    H = MLSTM_HEADS
    o_q = ATT_WIDTH
    o_k = o_q + ATT_KV_WIDTH
    o_v = o_k + ATT_KV_WIDTH
    o_qm = o_v + MLSTM_WIDTH
    o_km = o_qm + MLSTM_WIDTH
    o_vm = o_km + MLSTM_WIDTH
    o_om = o_vm + MLSTM_WIDTH
    o_g = o_om + 4 * H
    wg = w_in[:, o_om:o_g]
    order_i = jnp.concatenate([wg[:, 0:H], wg[:, 2 * H:3 * H]], axis=1)
    order_f = jnp.concatenate([wg[:, H:2 * H], wg[:, 3 * H:4 * H]], axis=1)
    gate_row = jnp.concatenate([order_i, order_f], axis=1).T
    b_i = jnp.concatenate([gate_bias[0:H], gate_bias[2 * H:3 * H]])
    b_f = jnp.concatenate([gate_bias[H:2 * H], gate_bias[3 * H:4 * H]])
    return {
        "qkv": w_in[:, :o_v].astype(BF16),
        "qkm": w_in[:, o_v:o_km].astype(BF16),
        "vo": w_in[:, o_km:o_om].astype(BF16),
        "gate_row": gate_row.astype(BF16),
        "bias_row": jnp.concatenate([b_i, b_f]).reshape(4 * H, 1).astype(F32),
        "merge": w_in[:, o_g:].astype(BF16),
        "q_gain": jnp.tile(q_gain, ATT_HEADS).reshape(1, ATT_WIDTH).astype(F32),
        "k_gain": jnp.tile(k_gain, ATT_KV_HEADS).reshape(1, ATT_KV_WIDTH).astype(F32),
    }


def kernel(x, positions, ffn1_norm, ffn1_w_gate, ffn1_w_up, ffn1_w_down, mix_norm, w_in, mlstm_gate_bias, attn_q_norm, attn_k_norm, attn_sink, mlstm_conv_w, mlstm_conv_b, mlstm_out_norm, w_branch_attn, w_branch_mlstm, w_out, ffn2_norm, ffn2_w_gate, ffn2_w_up, ffn2_w_down, block_out_norm):
    batch, seq, d = x.shape
    depth = w_in.shape[0]
    t = batch * seq
    xt = x.reshape(t, d)
    cos_t, sin_t = _rope_tables(positions)
    for l in range(depth):
        xt = _ffn(xt, ffn1_norm[l], ffn1_w_gate[l].astype(BF16), ffn1_w_up[l].astype(BF16),
                  ffn1_w_down[l].astype(BF16))
        pw = _proj_weights(w_in[l], mlstm_gate_bias[l], attn_q_norm[l], attn_k_norm[l])
        qa, k4, v4, qm, km, vm, og, grow, ga, gm = _proj(xt, mix_norm[l], pw, cos_t, sin_t)
        ya = _attention(qa, k4, v4, attn_sink[l].astype(F32), batch, seq)
        ym = _mlstm(qm, km, vm, og, grow, mlstm_conv_w[l].astype(F32),
                    mlstm_conv_b[l].reshape(1, -1).astype(F32),
                    mlstm_out_norm[l].reshape(1, -1).astype(F32), batch, seq)
        xt = _merge(xt, ya, ym, ga, gm, w_branch_attn[l].astype(BF16), w_branch_mlstm[l].astype(BF16),
                    w_out[l].astype(BF16))
        xt = _ffn(xt, ffn2_norm[l], ffn2_w_gate[l].astype(BF16), ffn2_w_up[l].astype(BF16),
                  ffn2_w_down[l].astype(BF16), final_gain=block_out_norm[l])
    return xt.reshape(batch, seq, d)
```

```python
import functools

import numpy as np
import jax
import jax.numpy as jnp
from jax import lax
from jax.experimental import pallas as pl
from jax.experimental.pallas import tpu as pltpu

F32 = jnp.float32
BF16 = jnp.bfloat16

ATT_HEAD_DIM = 64
ATT_HEADS = 8
ATT_KV_HEADS = 2
ATT_WIDTH = ATT_HEADS * ATT_HEAD_DIM
ATT_KV_WIDTH = ATT_KV_HEADS * ATT_HEAD_DIM
WINDOW = 128
ATT_BLOCK = 128
ROPE_DIM = ATT_HEAD_DIM // 4
ROPE_THETA = 500000.0
MLSTM_HEADS = 4
MLSTM_HEAD_DIM = 128
MLSTM_WIDTH = MLSTM_HEADS * MLSTM_HEAD_DIM
MLSTM_CHUNK = 128
NORM_EPS = 1e-6
NEG_BIG = -1e30
LOG2E = 1.4426950408889634

V7X_LANES = 128
V7X_MXU_COLS = 256
V7X_VMEM_LIMIT_BYTES = 56 * 1024 * 1024

FFN_ROWS = 1024
PROJ_ROWS = 512
MERGE_ROWS = 1024
CAST_ROWS = 256
ROPE_ROWS = 2048


def _params(n_axes):
    return pltpu.CompilerParams(
        dimension_semantics=("parallel",) * n_axes,
        vmem_limit_bytes=V7X_VMEM_LIMIT_BYTES,
    )


def _resident(shape):
    nd = len(shape)
    return pl.BlockSpec(shape, lambda *_: (0,) * nd, pipeline_mode=pl.Buffered(1))


def _layer(shape, layer):
    return pl.BlockSpec((None,) + tuple(shape[1:]), lambda *_: (layer, 0, 0), pipeline_mode=pl.Buffered(1))


def _rows(tm, width):
    return pl.BlockSpec((tm, width), lambda i: (i, 0))


def _cast_kernel(*refs):
    n = len(refs) // 2
    for src, dst in zip(refs[:n], refs[n:]):
        dst[...] = src[...].astype(BF16)


def _to_bf16(*ws):
    depth, rows, cols = ws[0].shape
    assert all(w.shape == ws[0].shape for w in ws) and rows % CAST_ROWS == 0
    spec = pl.BlockSpec((1, CAST_ROWS, cols), lambda l, i: (l, i, 0))
    return pl.pallas_call(
        _cast_kernel,
        out_shape=[jax.ShapeDtypeStruct(w.shape, BF16) for w in ws],
        grid=(depth, rows // CAST_ROWS),
        in_specs=[spec] * len(ws),
        out_specs=[spec] * len(ws),
        compiler_params=_params(2),
        name="cast_bf16",
    )(*ws)


def _rms(x, gain):
    ms = jnp.mean(x * x, axis=-1, keepdims=True)
    return x * lax.rsqrt(ms + NORM_EPS) * gain


def _dot(a, b):
    return jnp.dot(a, b, preferred_element_type=F32)


def _dot_nt(a, b):
    return lax.dot_general(a, b, (((1,), (1,)), ((), ())), preferred_element_type=F32)


def _dot_tn(a, b):
    return lax.dot_general(a, b, (((0,), (0,)), ((), ())), preferred_element_type=F32)


def _sigmoid(x):
    return 0.5 * jnp.tanh(0.5 * x) + 0.5


def _log_sigmoid(x):
    return jnp.minimum(x, 0.0) - jnp.log1p(jnp.exp(-jnp.abs(x)))


def _rope_kernel(pos_ref, freq_ref, cos_ref, sin_ref):
    ang = pos_ref[...].astype(F32) * freq_ref[...]
    j = lax.broadcasted_iota(jnp.int32, ang.shape, 1) % ATT_HEAD_DIM
    half = ROPE_DIM // 2
    c = jnp.cos(ang)
    s = jnp.sin(ang)
    cos_ref[...] = jnp.where(j < ROPE_DIM, c, 1.0)
    sin_ref[...] = jnp.where(j < half, -s, jnp.where(j < ROPE_DIM, s, 0.0))


def _rope_tables(positions):
    t = positions.size
    half = ROPE_DIM // 2
    inv_freq = np.power(np.float32(ROPE_THETA),
                        -np.arange(half, dtype=np.float32) * np.float32(2.0 / ROPE_DIM)).astype(np.float32)
    lane = np.arange(V7X_LANES) % ATT_HEAD_DIM
    freq = np.where(lane < ROPE_DIM, inv_freq[lane % half], 0.0).astype(np.float32)[None, :]
    pos = positions.reshape(t, 1)
    return pl.pallas_call(
        _rope_kernel,
        out_shape=[jax.ShapeDtypeStruct((t, V7X_LANES), F32)] * 2,
        grid=(t // ROPE_ROWS,),
        in_specs=[_rows(ROPE_ROWS, 1), _resident((1, V7X_LANES))],
        out_specs=[_rows(ROPE_ROWS, V7X_LANES)] * 2,
        compiler_params=_params(1),
        name="rope_tables",
    )(pos, jnp.asarray(freq))


def _ffn_kernel(*refs, d_ff, final_norm):
    if final_norm:
        x_ref, g_ref, wg_ref, wu_ref, wd_ref, go_ref, o_ref, hn_ref, a_ref = refs
    else:
        x_ref, g_ref, wg_ref, wu_ref, wd_ref, o_ref, hn_ref, a_ref = refs
    hn_ref[...] = _rms(x_ref[...], g_ref[...]).astype(BF16)
    for c0 in range(0, d_ff, V7X_MXU_COLS):
        cols = slice(c0, c0 + V7X_MXU_COLS)
        hn = hn_ref[...]
        gate = _dot(hn, wg_ref[:, cols])
        up = _dot(hn, wu_ref[:, cols])
        a_ref[:, cols] = (gate * _sigmoid(gate) * up).astype(BF16)
    out = x_ref[...] + 0.5 * _dot(a_ref[...], wd_ref[...])
    if final_norm:
        out = _rms(out, go_ref[...])
    o_ref[...] = out


def _ffn(x, gain, wg, wu, wd, layer, final_gain=None):
    t, d = x.shape
    d_ff = wg.shape[2]
    assert d_ff % V7X_MXU_COLS == 0 and t % FFN_ROWS == 0
    final_norm = final_gain is not None
    args = [x, gain.reshape(1, d), wg, wu, wd]
    in_specs = [_rows(FFN_ROWS, d), _resident((1, d)), _layer(wg.shape, layer),
                _layer(wu.shape, layer), _layer(wd.shape, layer)]
    if final_norm:
        args.append(final_gain.reshape(1, d))
        in_specs.append(_resident((1, d)))
    return pl.pallas_call(
        functools.partial(_ffn_kernel, d_ff=d_ff, final_norm=final_norm),
        out_shape=jax.ShapeDtypeStruct((t, d), F32),
        grid=(t // FFN_ROWS,),
        in_specs=in_specs,
        out_specs=_rows(FFN_ROWS, d),
        scratch_shapes=[pltpu.VMEM((FFN_ROWS, d), BF16), pltpu.VMEM((FFN_ROWS, d_ff), BF16)],
        compiler_params=_params(1),
        name="ffn_final" if final_norm else "ffn",
    )(*args)


def _head_rms(x, ones_bd, gain):
    ss = _dot((x * x).astype(BF16), ones_bd)
    return x * lax.rsqrt(ss * (1.0 / ATT_HEAD_DIM) + NORM_EPS) * gain


def _rope(x, cos, sin):
    n = x.shape[1]
    half = ROPE_DIM // 2
    j = lax.broadcasted_iota(jnp.int32, x.shape, 1) % ATT_HEAD_DIM
    partner = jnp.where(j < half, pltpu.roll(x, n - half, 1), pltpu.roll(x, half, 1))
    return x * cos + partner * sin


def _lo_hi(x):
    lo = lax.broadcasted_iota(jnp.int32, x.shape, 1) < ATT_HEAD_DIM
    xr = pltpu.roll(x, ATT_HEAD_DIM, 1)
    zero = jnp.zeros_like(x)
    return jnp.concatenate([jnp.where(lo, x, zero), jnp.where(lo, zero, xr),
                            jnp.where(lo, xr, zero), jnp.where(lo, zero, x)], axis=1)


def _proj_kernel(x_ref, g_ref, wqkv_ref, wqkm_ref, wvo_ref, wgr_ref, bgr_ref,
                 wmg_ref, qg_ref, kg_ref, cos_ref, sin_ref, ones_ref,
                 qa_ref, k4_ref, v4_ref, qm_ref, km_ref, vm_ref, og_ref, gr_ref,
                 ga_ref, gmm_ref):
    hn = _rms(x_ref[...], g_ref[...]).astype(BF16)

    def proj(w):
        return _dot(hn, w)

    cos = cos_ref[...]
    sin = sin_ref[...]
    ones_bd = ones_ref[...]

    q = proj(wqkv_ref[:, :ATT_WIDTH])
    k = proj(wqkv_ref[:, ATT_WIDTH:ATT_WIDTH + ATT_KV_WIDTH])
    v = proj(wqkv_ref[:, ATT_WIDTH + ATT_KV_WIDTH:])
    qm_ref[...] = proj(wqkm_ref[:, :MLSTM_WIDTH]).astype(BF16)
    q = _head_rms(q, ones_bd, qg_ref[...])
    km_ref[...] = proj(wqkm_ref[:, MLSTM_WIDTH:]).astype(BF16)
    k = _head_rms(k, ones_bd[:ATT_KV_WIDTH, :ATT_KV_WIDTH], kg_ref[...])
    vm_ref[...] = proj(wvo_ref[:, :MLSTM_WIDTH]).astype(BF16)
    og_ref[...] = _sigmoid(proj(wvo_ref[:, MLSTM_WIDTH:])).astype(BF16)
    gr = _dot_nt(wgr_ref[...], hn) + bgr_ref[...]
    for c in range(gr_ref.shape[0]):
        gr_ref[c] = gr[:, c * MLSTM_CHUNK:(c + 1) * MLSTM_CHUNK]

    reps = ATT_WIDTH // V7X_LANES
    q = _rope(q, jnp.concatenate([cos] * reps, axis=1), jnp.concatenate([sin] * reps, axis=1))
    qa_ref[...] = (q * (ATT_HEAD_DIM ** -0.5 * LOG2E)).astype(BF16)
    k4_ref[...] = _lo_hi(_rope(k, cos, sin)).astype(BF16)
    v4_ref[...] = _lo_hi(v).astype(BF16)

    d = ga_ref.shape[1]
    piece = 2 * V7X_MXU_COLS
    for c0 in range(0, d, piece):
        ga_ref[:, c0:c0 + piece] = _sigmoid(proj(wmg_ref[:, c0:c0 + piece])).astype(BF16)
    for c0 in range(0, d, piece):
        gmm_ref[:, c0:c0 + piece] = _sigmoid(proj(wmg_ref[:, d + c0:d + c0 + piece])).astype(BF16)


def _proj(x, gain, w, cos_t, sin_t):
    t, d = x.shape
    tm = PROJ_ROWS
    nck = tm // MLSTM_CHUNK
    heads = np.arange(ATT_WIDTH) // ATT_HEAD_DIM
    ones_bd = jnp.asarray((heads[:, None] == heads[None, :]).astype(np.float32), dtype=BF16)
    out_shape = [
        jax.ShapeDtypeStruct((t, ATT_WIDTH), BF16),
        jax.ShapeDtypeStruct((t, 4 * V7X_LANES), BF16),
        jax.ShapeDtypeStruct((t, 4 * V7X_LANES), BF16),
        jax.ShapeDtypeStruct((t, MLSTM_WIDTH), BF16),
        jax.ShapeDtypeStruct((t, MLSTM_WIDTH), BF16),
        jax.ShapeDtypeStruct((t, MLSTM_WIDTH), BF16),
        jax.ShapeDtypeStruct((t, MLSTM_WIDTH), BF16),
        jax.ShapeDtypeStruct((t // MLSTM_CHUNK, 16, MLSTM_CHUNK), F32),
        jax.ShapeDtypeStruct((t, d), BF16),
        jax.ShapeDtypeStruct((t, d), BF16),
    ]
    out_specs = [
        _rows(tm, ATT_WIDTH), _rows(tm, 4 * V7X_LANES), _rows(tm, 4 * V7X_LANES),
        _rows(tm, MLSTM_WIDTH), _rows(tm, MLSTM_WIDTH), _rows(tm, MLSTM_WIDTH), _rows(tm, MLSTM_WIDTH),
        pl.BlockSpec((nck, 16, MLSTM_CHUNK), lambda i: (i, 0, 0)),
        _rows(tm, d), _rows(tm, d),
    ]
    args = [x, gain.reshape(1, d), w["qkv"], w["qkm"], w["vo"], w["gate_row"],
            w["bias_row"], w["merge"], w["q_gain"], w["k_gain"], cos_t, sin_t, ones_bd]
    in_specs = [_rows(tm, d)] + [_resident(a.shape) for a in args[1:10]] + [
        _rows(tm, V7X_LANES), _rows(tm, V7X_LANES), _resident(ones_bd.shape)]
    return pl.pallas_call(
        _proj_kernel,
        out_shape=out_shape,
        grid=(t // tm,),
        in_specs=in_specs,
        out_specs=out_specs,
        compiler_params=_params(1),
        name="mixer_proj",
    )(*args)


def _attn_kernel(sink_ref, q_ref, k4_ref, v4_ref, o_ref, *, seq):
    blk = ATT_BLOCK
    assert WINDOW == blk
    nb = seq // blk
    group = ATT_HEADS // ATT_KV_HEADS

    def body(n, carry):
        r0 = pl.multiple_of(n * blk, blk)
        left = pl.multiple_of(jnp.maximum(r0 - blk, 0), blk)
        right = pl.multiple_of(jnp.minimum(r0 + blk, seq - blk), blk)
        qb = q_ref[pl.ds(r0, blk), :]
        kb = jnp.concatenate([k4_ref[pl.ds(left, blk), :], k4_ref[pl.ds(r0, blk), :],
                              k4_ref[pl.ds(right, blk), :]], axis=0)
        vb = jnp.concatenate([v4_ref[pl.ds(left, blk), :], v4_ref[pl.ds(r0, blk), :],
                              v4_ref[pl.ds(right, blk), :]], axis=0)
        ql = lax.broadcasted_iota(jnp.int32, (blk, blk), 0)
        kl = lax.broadcasted_iota(jnp.int32, (blk, blk), 1)
        left_ok = kl >= ql + jnp.where(n > 0, 0, blk)
        right_ok = kl + jnp.where(n < nb - 1, 0, blk) <= ql
        lo = lax.broadcasted_iota(jnp.int32, (blk, V7X_LANES), 1) < ATT_HEAD_DIM
        lo_band = lax.broadcasted_iota(jnp.int32, (3 * blk, V7X_LANES), 1) < ATT_HEAD_DIM
        ones_lo = jnp.where(lo_band, 1.0, 0.0).astype(BF16)
        ones_hi = jnp.where(lo_band, 0.0, 1.0).astype(BF16)
        outs = []
        for g in range(ATT_KV_HEADS):
            kv_lo = slice(2 * g * V7X_LANES, (2 * g + 1) * V7X_LANES)
            kv_hi = slice((2 * g + 1) * V7X_LANES, (2 * g + 2) * V7X_LANES)
            v_aug = jnp.concatenate([jnp.concatenate([vb[:, kv_lo], ones_lo], axis=1),
                                     jnp.concatenate([vb[:, kv_hi], ones_hi], axis=1)], axis=0)
            k_both = jnp.concatenate([kb[:, kv_lo], kb[:, kv_hi]], axis=0)
            for p in range(group // 2):
                pair = g * (group // 2) + p
                qp = qb[:, pair * V7X_LANES:(pair + 1) * V7X_LANES]
                s_both = _dot_nt(qp, k_both)
                probs, shifts = [], []
                for half in range(2):
                    sink = sink_ref[2 * pair + half] * LOG2E
                    s = s_both[:, half * 3 * blk:(half + 1) * 3 * blk]
                    s_l = jnp.where(left_ok, s[:, :blk], NEG_BIG)
                    s_m = s[:, blk:2 * blk]
                    s_r = jnp.where(right_ok, s[:, 2 * blk:], NEG_BIG)
                    m = jnp.max(jnp.maximum(jnp.maximum(s_l, s_r), s_m), axis=-1, keepdims=True)
                    m = jnp.maximum(m, sink)
                    probs += [jnp.exp2(s_l - m), jnp.exp2(s_m - m), jnp.exp2(s_r - m)]
                    shifts.append(sink - m)
                acc = _dot(jnp.concatenate(probs, axis=1).astype(BF16), v_aug)
                den = acc[:, V7X_LANES:] + jnp.exp2(jnp.where(lo, shifts[0], shifts[1]))
                outs.append(acc[:, :V7X_LANES] / den)
        o_ref[pl.ds(r0, blk), :] = jnp.concatenate(outs, axis=1).astype(BF16)
        return carry

    lax.fori_loop(0, nb, body, 0, unroll=8)


def _attention(q, k4, v4, sink, batch, seq):
    t = q.shape[0]
    by_batch = lambda w: pl.BlockSpec((seq, w), lambda b: (b, 0))
    return pl.pallas_call(
        functools.partial(_attn_kernel, seq=seq),
        out_shape=jax.ShapeDtypeStruct((t, ATT_WIDTH), BF16),
        grid=(batch,),
        in_specs=[pl.BlockSpec(memory_space=pltpu.SMEM), by_batch(ATT_WIDTH),
                  by_batch(4 * V7X_LANES), by_batch(4 * V7X_LANES)],
        out_specs=by_batch(ATT_WIDTH),
        compiler_params=_params(1),
        name="window_attn",
    )(sink, q, k4, v4)


def _scan(x, axis, reverse, op, identity):
    n = x.shape[axis]
    idx = lax.broadcasted_iota(jnp.int32, x.shape, axis)
    d = 1
    while d < n:
        if reverse:
            x = op(x, jnp.where(idx < n - d, pltpu.roll(x, n - d, axis), identity))
        else:
            x = op(x, jnp.where(idx >= d, pltpu.roll(x, d, axis), identity))
        d *= 2
    return x


def _mlstm_kernel(qraw_ref, kraw_ref, v_ref, og_ref, gr_ref, cw_ref, cb_ref, ng_ref,
                  y_ref,
                  ks_ref, qt_ref, kt_ref, vt_ref, rcol_ref, b_ref, dm_ref, w_ref, r_ref, am_ref, bt_ref,
                  cst_ref, mst_ref, cin_ref, minf_ref, minb_ref, hs_ref, *, seq):
    L = MLSTM_CHUNK
    H = MLSTM_HEADS
    dh = MLSTM_HEAD_DIM
    nc = seq // L
    width = MLSTM_WIDTH
    halo = 16
    row8 = lax.broadcasted_iota(jnp.int32, (8, L), 0)

    def head(h):
        return slice(h * dh, (h + 1) * dh)

    assert nc * 8 == V7X_LANES and L == V7X_LANES
    shape = (nc * 8, L)
    gi = gr_ref[:, 0:8, :].reshape(shape)
    lf = _log_sigmoid(gr_ref[:, 8:16, :].reshape(shape))
    fwd = (lax.broadcasted_iota(jnp.int32, shape, 0) & 7) < H
    lane = lax.broadcasted_iota(jnp.int32, shape, 1)
    b = jnp.where(fwd, _scan(lf, 1, False, jnp.add, 0.0), _scan(lf, 1, True, jnp.add, 0.0))
    b_tot = jnp.sum(jnp.where(lane == jnp.where(fwd, L - 1, 0), b, 0.0), axis=1, keepdims=True)
    a = b_tot - b + gi
    a_max = jnp.max(a, axis=1, keepdims=True)
    r = gi - b
    d_max = b + jnp.where(fwd, _scan(r, 1, False, jnp.maximum, NEG_BIG),
                          _scan(r, 1, True, jnp.maximum, NEG_BIG))
    r_ref[...] = (r * LOG2E).reshape(nc, 8, L)
    b_ref[...] = (b * LOG2E).reshape(nc, 8, L)
    dm_ref[...] = (d_max * LOG2E).reshape(nc, 8, L)
    w_ref[...] = jnp.exp(a - a_max).reshape(nc, 8, L)
    am_ref[...] = jnp.broadcast_to(a_max, shape).reshape(nc, 8, L)
    bt_ref[...] = jnp.broadcast_to(b_tot, shape).reshape(nc, 8, L)

    def prep(c, carry):
        r0 = pl.multiple_of(c * L, L)
        rows = pl.ds(r0, L)
        rid = lax.broadcasted_iota(jnp.int32, (L, width), 0)
        has_prev = jnp.where(c > 0, 1.0, 0.0)
        has_next = jnp.where(c < nc - 1, 1.0, 0.0)
        prev_at = pl.multiple_of(jnp.maximum(r0 - halo, 0), halo)
        next_at = pl.multiple_of(jnp.minimum(r0 + L, seq - halo), halo)
        acts = []
        for idx, raw_ref in enumerate((qraw_ref, kraw_ref)):
            cols = slice(idx * width, (idx + 1) * width)
            cur = raw_ref[rows, :].astype(F32)
            prev_row = raw_ref[pl.ds(prev_at, halo), :].astype(F32)[halo - 1:halo, :] * has_prev
            next_row = raw_ref[pl.ds(next_at, halo), :].astype(F32)[0:1, :] * has_next
            before = jnp.where(rid == 0, prev_row, pltpu.roll(cur, 1, 0))
            after = jnp.where(rid == L - 1, next_row, pltpu.roll(cur, L - 1, 0))
            u = (before * cw_ref[0:1, cols] + cur * cw_ref[1:2, cols] + after * cw_ref[2:3, cols]
                 + cb_ref[:, cols])
            u = u / (1.0 + jnp.exp2(u * (-LOG2E)))
            if idx == 1:
                u = u * (dh ** -0.5)
            acts.append(u)
        q_act, k_act = acts
        ks_ref[rows, :] = k_act.astype(BF16)
        ones_rows = jnp.ones((halo, L), BF16)
        for h in range(H):
            qt_ref[c, h] = q_act[:, head(h)].T.astype(BF16)
            kt_ref[c, h] = k_act[:, head(h)].T.astype(BF16)
            vt_ref[c, h, 0:dh, :] = v_ref[rows, head(h)].astype(F32).T.astype(BF16)
            vt_ref[c, h, dh:dh + halo, :] = ones_rows
        rcol_ref[rows, :] = jnp.concatenate([r_ref[c], jnp.zeros((L - 8, L), F32)], axis=0).T
        return carry

    lax.fori_loop(0, nc, prep, 0, unroll=2)

    cst_ref[...] = jnp.zeros_like(cst_ref)
    mst_ref[...] = jnp.zeros_like(mst_ref)

    def scan(i, carry):
        c_f = i
        c_b = nc - 1 - i
        fwd = row8 < H
        bt = jnp.where(fwd, bt_ref[c_f], bt_ref[c_b])
        am = jnp.where(fwd, am_ref[c_f], am_ref[c_b])
        m_old = mst_ref[...]
        m_new = jnp.maximum(bt + m_old, am)
        s_prev = jnp.exp(bt + m_old - m_new)
        s_loc = jnp.exp(am - m_new)
        minf_ref[c_f] = m_old
        minb_ref[c_b] = m_old
        mst_ref[...] = m_new
        for d, c in enumerate((c_f, c_b)):
            wr = w_ref[c]
            for h in range(H):
                j = d * H + h
                kw_t = (kt_ref[c, h].astype(F32) * wr[j:j + 1, :]).astype(BF16)
                c_loc = _dot_nt(vt_ref[c, h], kw_t)
                c_old = cst_ref[j]
                cin_ref[c, j] = c_old.astype(BF16)
                cst_ref[j] = s_prev[j:j + 1, :] * c_old + s_loc[j:j + 1, :] * c_loc
        return carry

    lax.fori_loop(0, nc, scan, 0, unroll=4)

    def emit(c):
        rows = pl.ds(pl.multiple_of(c * L, L), L)
        m_in = jnp.where(row8 < H, minf_ref[c], minb_ref[c]) * LOG2E
        bc = b_ref[c]
        m_t = jnp.maximum(bc + m_in, dm_ref[c])
        b_rel = bc - m_t
        scale_in = jnp.exp2(b_rel + m_in)
        floor = jnp.exp2(-m_t)
        rcol = rcol_ref[rows, :]
        s_id = lax.broadcasted_iota(jnp.int32, (L, L), 0)
        t_id = lax.broadcasted_iota(jnp.int32, (L, L), 1)
        masks = (s_id <= t_id, s_id >= t_id)
        for h in range(H):
            q_t = qt_ref[c, h]
            s_t = _dot(ks_ref[rows, head(h)], q_t)
            weights, sums, inters = [], [], []
            for d in range(2):
                j = d * H + h
                logw = jnp.broadcast_to(rcol[:, j:j + 1], (L, L)) + b_rel[j:j + 1, :]
                sc = s_t * jnp.exp2(jnp.where(masks[d], logw, NEG_BIG))
                sums.append(jnp.sum(sc, axis=0, keepdims=True))
                weights.append(sc.astype(BF16))
                inters.append(_dot(cin_ref[c, j], q_t))
            num = _dot(vt_ref[c, h, 0:dh, :], jnp.concatenate(weights, axis=1))
            hsum = None
            for d in range(2):
                j = d * H + h
                den = sums[d] + scale_in[j:j + 1, :] * inters[d][dh:dh + 1, :]
                inv = 1.0 / jnp.maximum(jnp.abs(den), floor[j:j + 1, :])
                hd = num[:, d * L:(d + 1) * L] * inv + inters[d][0:dh, :] * (scale_in[j:j + 1, :] * inv)
                hsum = hd if hsum is None else hsum + hd
            hs_ref[c, h] = hsum

    def finish(c):
        rows = pl.ds(pl.multiple_of(c * L, L), L)
        for h in range(H):
            hsum = hs_ref[c, h]
            xc = hsum - jnp.mean(hsum, axis=0, keepdims=True)
            var = jnp.mean(xc * xc, axis=0, keepdims=True)
            y = (xc * lax.rsqrt(var + NORM_EPS)).T * ng_ref[:, head(h)] * og_ref[rows, head(h)].astype(F32)
            y_ref[rows, head(h)] = y.astype(BF16)

    def emit_and_finish(c, carry):
        finish(c - 1)
        emit(c)
        return carry

    emit(0)
    lax.fori_loop(1, nc, emit_and_finish, 0, unroll=3)
    finish(nc - 1)


def _mlstm(qraw, kraw, v, og, grow, conv_w, conv_b, norm_g, batch, seq):
    t = qraw.shape[0]
    nc = seq // MLSTM_CHUNK
    nd = 2 * MLSTM_HEADS
    by_batch = lambda w: pl.BlockSpec((seq, w), lambda b: (b, 0))
    dh = MLSTM_HEAD_DIM
    aug = dh + 16
    tile = (nc, MLSTM_HEADS, dh, MLSTM_CHUNK)
    rowvec = pltpu.VMEM((nc, 8, MLSTM_CHUNK), F32)
    scratch = [
        pltpu.VMEM((seq, MLSTM_WIDTH), BF16),
        pltpu.VMEM(tile, BF16),
        pltpu.VMEM(tile, BF16),
        pltpu.VMEM((nc, MLSTM_HEADS, aug, MLSTM_CHUNK), BF16),
        pltpu.VMEM((seq, V7X_LANES), F32),
        rowvec,
        rowvec,
        rowvec,
        rowvec,
        rowvec,
        rowvec,
        pltpu.VMEM((nd, aug, dh), F32),
        pltpu.VMEM((8, MLSTM_CHUNK), F32),
        pltpu.VMEM((nc, nd, aug, dh), BF16),
        rowvec,
        rowvec,
        pltpu.VMEM(tile, F32),
    ]
    return pl.pallas_call(
        functools.partial(_mlstm_kernel, seq=seq),
        out_shape=jax.ShapeDtypeStruct((t, MLSTM_WIDTH), BF16),
        grid=(batch,),
        in_specs=[by_batch(MLSTM_WIDTH)] * 4 + [
            pl.BlockSpec((nc, 16, MLSTM_CHUNK), lambda b: (b, 0, 0)),
            _resident(conv_w.shape), _resident(conv_b.shape), _resident(norm_g.shape)],
        out_specs=by_batch(MLSTM_WIDTH),
        scratch_shapes=scratch,
        compiler_params=_params(1),
        name="bidir_mlstm",
    )(qraw, kraw, v, og, grow, conv_w, conv_b, norm_g)


def _merge_kernel(x_ref, ya_ref, ym_ref, ga_ref, gm_ref, wa_ref, wb_ref, wo_ref, o_ref):
    merged = (ga_ref[...].astype(F32) * _dot(ya_ref[...], wa_ref[...])
              + gm_ref[...].astype(F32) * _dot(ym_ref[...], wb_ref[...]))
    o_ref[...] = x_ref[...] + _dot(merged.astype(BF16), wo_ref[...])


def _merge(x, ya, ym, ga, gm, wa, wb, wo, layer):
    t, d = x.shape
    tm = MERGE_ROWS
    return pl.pallas_call(
        _merge_kernel,
        out_shape=jax.ShapeDtypeStruct((t, d), F32),
        grid=(t // tm,),
        in_specs=[_rows(tm, d), _rows(tm, ATT_WIDTH), _rows(tm, MLSTM_WIDTH), _rows(tm, d), _rows(tm, d),
                  _layer(wa.shape, layer), _layer(wb.shape, layer), _layer(wo.shape, layer)],
        out_specs=_rows(tm, d),
        compiler_params=_params(1),
        name="merge_out",
    )(x, ya, ym, ga, gm, wa, wb, wo)


def _proj_weights(w_in, gate_bias, q_gain, k_gain):
    d = w_in.shape[0]
    H = MLSTM_HEADS
    o_q = ATT_WIDTH
    o_k = o_q + ATT_KV_WIDTH
    o_v = o_k + ATT_KV_WIDTH
    o_qm = o_v + MLSTM_WIDTH
    o_km = o_qm + MLSTM_WIDTH
    o_vm = o_km + MLSTM_WIDTH
    o_om = o_vm + MLSTM_WIDTH
    o_g = o_om + 4 * H
    wg = w_in[:, o_om:o_g]
    order_i = jnp.concatenate([wg[:, 0:H], wg[:, 2 * H:3 * H]], axis=1)
    order_f = jnp.concatenate([wg[:, H:2 * H], wg[:, 3 * H:4 * H]], axis=1)
    gate_row = jnp.concatenate([order_i, order_f], axis=1).T
    b_i = jnp.concatenate([gate_bias[0:H], gate_bias[2 * H:3 * H]])
    b_f = jnp.concatenate([gate_bias[H:2 * H], gate_bias[3 * H:4 * H]])
    return {
        "qkv": w_in[:, :o_v].astype(BF16),
        "qkm": w_in[:, o_v:o_km].astype(BF16),
        "vo": w_in[:, o_km:o_om].astype(BF16),
        "gate_row": gate_row.astype(BF16),
        "bias_row": jnp.concatenate([b_i, b_f]).reshape(4 * H, 1).astype(F32),
        "merge": w_in[:, o_g:].astype(BF16),
        "q_gain": jnp.tile(q_gain, ATT_HEADS).reshape(1, ATT_WIDTH).astype(F32),
        "k_gain": jnp.tile(k_gain, ATT_KV_HEADS).reshape(1, ATT_KV_WIDTH).astype(F32),
    }


def kernel(x, positions, ffn1_norm, ffn1_w_gate, ffn1_w_up, ffn1_w_down, mix_norm, w_in, mlstm_gate_bias, attn_q_norm, attn_k_norm, attn_sink, mlstm_conv_w, mlstm_conv_b, mlstm_out_norm, w_branch_attn, w_branch_mlstm, w_out, ffn2_norm, ffn2_w_gate, ffn2_w_up, ffn2_w_down, block_out_norm):
    batch, seq, d = x.shape
    depth = w_in.shape[0]
    t = batch * seq
    xt = x.reshape(t, d)
    cos_t, sin_t = _rope_tables(positions)
    wg1, wu1, wg2, wu2 = _to_bf16(ffn1_w_gate, ffn1_w_up, ffn2_w_gate, ffn2_w_up)
    wd1, wd2 = _to_bf16(ffn1_w_down, ffn2_w_down)
    wa, wb = _to_bf16(w_branch_attn, w_branch_mlstm)
    (wo,) = _to_bf16(w_out)
    (win,) = _to_bf16(w_in)
    for l in range(depth):
        xt = _ffn(xt, ffn1_norm[l], wg1, wu1, wd1, l)
        pw = _proj_weights(win[l], mlstm_gate_bias[l], attn_q_norm[l], attn_k_norm[l])
        qa, k4, v4, qm, km, vm, og, grow, ga, gm = _proj(xt, mix_norm[l], pw, cos_t, sin_t)
        ya = _attention(qa, k4, v4, attn_sink[l].astype(F32), batch, seq)
        ym = _mlstm(qm, km, vm, og, grow, mlstm_conv_w[l].astype(F32),
                    mlstm_conv_b[l].reshape(1, -1).astype(F32),
                    mlstm_out_norm[l].reshape(1, -1).astype(F32), batch, seq)
        xt = _merge(xt, ya, ym, ga, gm, wa, wb, wo, l)
        xt = _ffn(xt, ffn2_norm[l], wg2, wu2, wd2, l, final_gain=block_out_norm[l])
    return xt.reshape(batch, seq, d)
```

```python
import functools

import numpy as np
import jax
import jax.numpy as jnp
from jax import lax
from jax.experimental import pallas as pl
from jax.experimental.pallas import tpu as pltpu

F32 = jnp.float32
BF16 = jnp.bfloat16

ATT_HEAD_DIM = 64
ATT_HEADS = 8
ATT_KV_HEADS = 2
ATT_WIDTH = ATT_HEADS * ATT_HEAD_DIM
ATT_KV_WIDTH = ATT_KV_HEADS * ATT_HEAD_DIM
WINDOW = 128
ATT_BLOCK = 128
ROPE_DIM = ATT_HEAD_DIM // 4
ROPE_THETA = 500000.0
MLSTM_HEADS = 4
MLSTM_HEAD_DIM = 128
MLSTM_WIDTH = MLSTM_HEADS * MLSTM_HEAD_DIM
MLSTM_CHUNK = 128
NORM_EPS = 1e-6
NEG_BIG = -1e30
LOG2E = 1.4426950408889634

V7X_LANES = 128
V7X_MXU_COLS = 256
V7X_VMEM_LIMIT_BYTES = 56 * 1024 * 1024

FFN_ROWS = 1024
PROJ_ROWS = 512
MERGE_ROWS = 1024
CAST_ROWS = 256
ROPE_ROWS = 2048


def _params(n_axes):
    return pltpu.CompilerParams(
        dimension_semantics=("parallel",) * n_axes,
        vmem_limit_bytes=V7X_VMEM_LIMIT_BYTES,
    )


def _resident(shape):
    nd = len(shape)
    return pl.BlockSpec(shape, lambda *_: (0,) * nd, pipeline_mode=pl.Buffered(1))


def _layer(shape, layer):
    return pl.BlockSpec((None,) + tuple(shape[1:]), lambda *_: (layer, 0, 0), pipeline_mode=pl.Buffered(1))


def _rows(tm, width):
    return pl.BlockSpec((tm, width), lambda i: (i, 0))


def _cast_kernel(*refs):
    n = len(refs) // 2
    for src, dst in zip(refs[:n], refs[n:]):
        dst[...] = src[...].astype(BF16)


def _to_bf16(*ws):
    depth, rows, cols = ws[0].shape
    assert all(w.shape == ws[0].shape for w in ws) and rows % CAST_ROWS == 0
    spec = pl.BlockSpec((1, CAST_ROWS, cols), lambda l, i: (l, i, 0))
    return pl.pallas_call(
        _cast_kernel,
        out_shape=[jax.ShapeDtypeStruct(w.shape, BF16) for w in ws],
        grid=(depth, rows // CAST_ROWS),
        in_specs=[spec] * len(ws),
        out_specs=[spec] * len(ws),
        compiler_params=_params(2),
        name="cast_bf16",
    )(*ws)


def _rms(x, gain):
    ms = jnp.mean(x * x, axis=-1, keepdims=True)
    return x * lax.rsqrt(ms + NORM_EPS) * gain


def _dot(a, b):
    return jnp.dot(a, b, preferred_element_type=F32)


def _dot_nt(a, b):
    return lax.dot_general(a, b, (((1,), (1,)), ((), ())), preferred_element_type=F32)


def _dot_tn(a, b):
    return lax.dot_general(a, b, (((0,), (0,)), ((), ())), preferred_element_type=F32)


def _sigmoid(x):
    return 0.5 * jnp.tanh(0.5 * x) + 0.5


def _log_sigmoid(x):
    return jnp.minimum(x, 0.0) - jnp.log1p(jnp.exp(-jnp.abs(x)))


def _rope_kernel(pos_ref, freq_ref, cos_ref, sin_ref):
    ang = pos_ref[...].astype(F32) * freq_ref[...]
    j = lax.broadcasted_iota(jnp.int32, ang.shape, 1) % ATT_HEAD_DIM
    half = ROPE_DIM // 2
    c = jnp.cos(ang)
    s = jnp.sin(ang)
    cos_ref[...] = jnp.where(j < ROPE_DIM, c, 1.0)
    sin_ref[...] = jnp.where(j < half, -s, jnp.where(j < ROPE_DIM, s, 0.0))


def _rope_tables(positions):
    t = positions.size
    half = ROPE_DIM // 2
    inv_freq = np.power(np.float32(ROPE_THETA),
                        -np.arange(half, dtype=np.float32) * np.float32(2.0 / ROPE_DIM)).astype(np.float32)
    lane = np.arange(V7X_LANES) % ATT_HEAD_DIM
    freq = np.where(lane < ROPE_DIM, inv_freq[lane % half], 0.0).astype(np.float32)[None, :]
    pos = positions.reshape(t, 1)
    return pl.pallas_call(
        _rope_kernel,
        out_shape=[jax.ShapeDtypeStruct((t, V7X_LANES), F32)] * 2,
        grid=(t // ROPE_ROWS,),
        in_specs=[_rows(ROPE_ROWS, 1), _resident((1, V7X_LANES))],
        out_specs=[_rows(ROPE_ROWS, V7X_LANES)] * 2,
        compiler_params=_params(1),
        name="rope_tables",
    )(pos, jnp.asarray(freq))


def _ffn_kernel(*refs, d_ff, final_norm):
    if final_norm:
        x_ref, g_ref, wg_ref, wu_ref, wd_ref, go_ref, o_ref, hn_ref, a_ref = refs
    else:
        x_ref, g_ref, wg_ref, wu_ref, wd_ref, o_ref, hn_ref, a_ref = refs
    hn_ref[...] = _rms(x_ref[...], g_ref[...]).astype(BF16)
    for c0 in range(0, d_ff, V7X_MXU_COLS):
        cols = slice(c0, c0 + V7X_MXU_COLS)
        hn = hn_ref[...]
        gate = _dot(hn, wg_ref[:, cols])
        up = _dot(hn, wu_ref[:, cols])
        a_ref[:, cols] = (gate * _sigmoid(gate) * up).astype(BF16)
    out = x_ref[...] + 0.5 * _dot(a_ref[...], wd_ref[...])
    if final_norm:
        out = _rms(out, go_ref[...])
    o_ref[...] = out


def _ffn(x, gain, wg, wu, wd, layer, final_gain=None):
    t, d = x.shape
    d_ff = wg.shape[2]
    assert d_ff % V7X_MXU_COLS == 0 and t % FFN_ROWS == 0
    final_norm = final_gain is not None
    args = [x, gain.reshape(1, d), wg, wu, wd]
    in_specs = [_rows(FFN_ROWS, d), _resident((1, d)), _layer(wg.shape, layer),
                _layer(wu.shape, layer), _layer(wd.shape, layer)]
    if final_norm:
        args.append(final_gain.reshape(1, d))
        in_specs.append(_resident((1, d)))
    return pl.pallas_call(
        functools.partial(_ffn_kernel, d_ff=d_ff, final_norm=final_norm),
        out_shape=jax.ShapeDtypeStruct((t, d), F32),
        grid=(t // FFN_ROWS,),
        in_specs=in_specs,
        out_specs=_rows(FFN_ROWS, d),
        scratch_shapes=[pltpu.VMEM((FFN_ROWS, d), BF16), pltpu.VMEM((FFN_ROWS, d_ff), BF16)],
        compiler_params=_params(1),
        name="ffn_final" if final_norm else "ffn",
    )(*args)


def _head_rms(x, ones_bd, gain):
    ss = _dot((x * x).astype(BF16), ones_bd)
    return x * lax.rsqrt(ss * (1.0 / ATT_HEAD_DIM) + NORM_EPS) * gain


def _rope(x, cos, sin):
    n = x.shape[1]
    half = ROPE_DIM // 2
    j = lax.broadcasted_iota(jnp.int32, x.shape, 1) % ATT_HEAD_DIM
    partner = jnp.where(j < half, pltpu.roll(x, n - half, 1), pltpu.roll(x, half, 1))
    return x * cos + partner * sin


def _lo_hi(x):
    lo = lax.broadcasted_iota(jnp.int32, x.shape, 1) < ATT_HEAD_DIM
    xr = pltpu.roll(x, ATT_HEAD_DIM, 1)
    zero = jnp.zeros_like(x)
    return jnp.concatenate([jnp.where(lo, x, zero), jnp.where(lo, zero, xr),
                            jnp.where(lo, xr, zero), jnp.where(lo, zero, x)], axis=1)


def _proj_kernel(x_ref, g_ref, wqkv_ref, wqkm_ref, wvo_ref, wgr_ref, bgr_ref,
                 wmg_ref, qg_ref, kg_ref, cos_ref, sin_ref, ones_ref,
                 qa_ref, k4_ref, v4_ref, qm_ref, km_ref, vm_ref, og_ref, gr_ref,
                 ga_ref, gmm_ref):
    hn = _rms(x_ref[...], g_ref[...]).astype(BF16)

    def proj(w):
        return _dot(hn, w)

    cos = cos_ref[...]
    sin = sin_ref[...]
    ones_bd = ones_ref[...]

    q = proj(wqkv_ref[:, :ATT_WIDTH])
    k = proj(wqkv_ref[:, ATT_WIDTH:ATT_WIDTH + ATT_KV_WIDTH])
    v = proj(wqkv_ref[:, ATT_WIDTH + ATT_KV_WIDTH:])
    qm_ref[...] = proj(wqkm_ref[:, :MLSTM_WIDTH]).astype(BF16)
    q = _head_rms(q, ones_bd, qg_ref[...])
    km_ref[...] = proj(wqkm_ref[:, MLSTM_WIDTH:]).astype(BF16)
    k = _head_rms(k, ones_bd[:ATT_KV_WIDTH, :ATT_KV_WIDTH], kg_ref[...])
    vm_ref[...] = proj(wvo_ref[:, :MLSTM_WIDTH]).astype(BF16)
    og_ref[...] = _sigmoid(proj(wvo_ref[:, MLSTM_WIDTH:])).astype(BF16)
    gr = _dot_nt(wgr_ref[...], hn) + bgr_ref[...]
    for c in range(gr_ref.shape[0]):
        gr_ref[c] = gr[:, c * MLSTM_CHUNK:(c + 1) * MLSTM_CHUNK]

    reps = ATT_WIDTH // V7X_LANES
    q = _rope(q, jnp.concatenate([cos] * reps, axis=1), jnp.concatenate([sin] * reps, axis=1))
    qa_ref[...] = (q * (ATT_HEAD_DIM ** -0.5 * LOG2E)).astype(BF16)
    k4_ref[...] = _lo_hi(_rope(k, cos, sin)).astype(BF16)
    v4_ref[...] = _lo_hi(v).astype(BF16)

    d = ga_ref.shape[1]
    piece = 2 * V7X_MXU_COLS
    for c0 in range(0, d, piece):
        ga_ref[:, c0:c0 + piece] = _sigmoid(proj(wmg_ref[:, c0:c0 + piece])).astype(BF16)
    for c0 in range(0, d, piece):
        gmm_ref[:, c0:c0 + piece] = _sigmoid(proj(wmg_ref[:, d + c0:d + c0 + piece])).astype(BF16)


def _proj(x, gain, big, layer, w, cos_t, sin_t):
    t, d = x.shape
    tm = PROJ_ROWS
    nck = tm // MLSTM_CHUNK
    heads = np.arange(ATT_WIDTH) // ATT_HEAD_DIM
    ones_bd = jnp.asarray((heads[:, None] == heads[None, :]).astype(np.float32), dtype=BF16)
    out_shape = [
        jax.ShapeDtypeStruct((t, ATT_WIDTH), BF16),
        jax.ShapeDtypeStruct((t, 4 * V7X_LANES), BF16),
        jax.ShapeDtypeStruct((t, 4 * V7X_LANES), BF16),
        jax.ShapeDtypeStruct((t, MLSTM_WIDTH), BF16),
        jax.ShapeDtypeStruct((t, MLSTM_WIDTH), BF16),
        jax.ShapeDtypeStruct((t, MLSTM_WIDTH), BF16),
        jax.ShapeDtypeStruct((t, MLSTM_WIDTH), BF16),
        jax.ShapeDtypeStruct((t // MLSTM_CHUNK, 16, MLSTM_CHUNK), F32),
        jax.ShapeDtypeStruct((t, d), BF16),
        jax.ShapeDtypeStruct((t, d), BF16),
    ]
    out_specs = [
        _rows(tm, ATT_WIDTH), _rows(tm, 4 * V7X_LANES), _rows(tm, 4 * V7X_LANES),
        _rows(tm, MLSTM_WIDTH), _rows(tm, MLSTM_WIDTH), _rows(tm, MLSTM_WIDTH), _rows(tm, MLSTM_WIDTH),
        pl.BlockSpec((nck, 16, MLSTM_CHUNK), lambda i: (i, 0, 0)),
        _rows(tm, d), _rows(tm, d),
    ]
    args = [x, gain.reshape(1, d), big["qkv"], big["qkm"], big["vo"], w["gate_row"],
            w["bias_row"], big["merge"], w["q_gain"], w["k_gain"], cos_t, sin_t, ones_bd]
    in_specs = [_rows(tm, d)] + [
        _layer(a.shape, layer) if a.ndim == 3 else _resident(a.shape) for a in args[1:10]] + [
        _rows(tm, V7X_LANES), _rows(tm, V7X_LANES), _resident(ones_bd.shape)]
    return pl.pallas_call(
        _proj_kernel,
        out_shape=out_shape,
        grid=(t // tm,),
        in_specs=in_specs,
        out_specs=out_specs,
        compiler_params=_params(1),
        name="mixer_proj",
    )(*args)


def _attn_kernel(sink_ref, q_ref, k4_ref, v4_ref, o_ref, *, seq):
    blk = ATT_BLOCK
    assert WINDOW == blk
    nb = seq // blk
    group = ATT_HEADS // ATT_KV_HEADS

    def body(n, carry):
        r0 = pl.multiple_of(n * blk, blk)
        left = pl.multiple_of(jnp.maximum(r0 - blk, 0), blk)
        right = pl.multiple_of(jnp.minimum(r0 + blk, seq - blk), blk)
        qb = q_ref[pl.ds(r0, blk), :]
        kb = jnp.concatenate([k4_ref[pl.ds(left, blk), :], k4_ref[pl.ds(r0, blk), :],
                              k4_ref[pl.ds(right, blk), :]], axis=0)
        vb = jnp.concatenate([v4_ref[pl.ds(left, blk), :], v4_ref[pl.ds(r0, blk), :],
                              v4_ref[pl.ds(right, blk), :]], axis=0)
        ql = lax.broadcasted_iota(jnp.int32, (blk, blk), 0)
        kl = lax.broadcasted_iota(jnp.int32, (blk, blk), 1)
        left_ok = kl >= ql + jnp.where(n > 0, 0, blk)
        right_ok = kl + jnp.where(n < nb - 1, 0, blk) <= ql
        lo = lax.broadcasted_iota(jnp.int32, (blk, V7X_LANES), 1) < ATT_HEAD_DIM
        lo_band = lax.broadcasted_iota(jnp.int32, (3 * blk, V7X_LANES), 1) < ATT_HEAD_DIM
        ones_lo = jnp.where(lo_band, 1.0, 0.0).astype(BF16)
        ones_hi = jnp.where(lo_band, 0.0, 1.0).astype(BF16)
        outs = []
        for g in range(ATT_KV_HEADS):
            kv_lo = slice(2 * g * V7X_LANES, (2 * g + 1) * V7X_LANES)
            kv_hi = slice((2 * g + 1) * V7X_LANES, (2 * g + 2) * V7X_LANES)
            v_aug = jnp.concatenate([jnp.concatenate([vb[:, kv_lo], ones_lo], axis=1),
                                     jnp.concatenate([vb[:, kv_hi], ones_hi], axis=1)], axis=0)
            k_both = jnp.concatenate([kb[:, kv_lo], kb[:, kv_hi]], axis=0)
            for p in range(group // 2):
                pair = g * (group // 2) + p
                qp = qb[:, pair * V7X_LANES:(pair + 1) * V7X_LANES]
                s_both = _dot_nt(qp, k_both)
                probs, shifts = [], []
                for half in range(2):
                    sink = sink_ref[2 * pair + half] * LOG2E
                    s = s_both[:, half * 3 * blk:(half + 1) * 3 * blk]
                    s_l = jnp.where(left_ok, s[:, :blk], NEG_BIG)
                    s_m = s[:, blk:2 * blk]
                    s_r = jnp.where(right_ok, s[:, 2 * blk:], NEG_BIG)
                    m = jnp.max(jnp.maximum(jnp.maximum(s_l, s_r), s_m), axis=-1, keepdims=True)
                    m = jnp.maximum(m, sink)
                    probs += [jnp.exp2(s_l - m), jnp.exp2(s_m - m), jnp.exp2(s_r - m)]
                    shifts.append(sink - m)
                acc = _dot(jnp.concatenate(probs, axis=1).astype(BF16), v_aug)
                den = acc[:, V7X_LANES:] + jnp.exp2(jnp.where(lo, shifts[0], shifts[1]))
                outs.append(acc[:, :V7X_LANES] / den)
        o_ref[pl.ds(r0, blk), :] = jnp.concatenate(outs, axis=1).astype(BF16)
        return carry

    lax.fori_loop(0, nb, body, 0, unroll=8)


def _attention(q, k4, v4, sink, batch, seq):
    t = q.shape[0]
    by_batch = lambda w: pl.BlockSpec((seq, w), lambda b: (b, 0))
    return pl.pallas_call(
        functools.partial(_attn_kernel, seq=seq),
        out_shape=jax.ShapeDtypeStruct((t, ATT_WIDTH), BF16),
        grid=(batch,),
        in_specs=[pl.BlockSpec(memory_space=pltpu.SMEM), by_batch(ATT_WIDTH),
                  by_batch(4 * V7X_LANES), by_batch(4 * V7X_LANES)],
        out_specs=by_batch(ATT_WIDTH),
        compiler_params=_params(1),
        name="window_attn",
    )(sink, q, k4, v4)


def _scan(x, axis, reverse, op, identity):
    n = x.shape[axis]
    idx = lax.broadcasted_iota(jnp.int32, x.shape, axis)
    d = 1
    while d < n:
        if reverse:
            x = op(x, jnp.where(idx < n - d, pltpu.roll(x, n - d, axis), identity))
        else:
            x = op(x, jnp.where(idx >= d, pltpu.roll(x, d, axis), identity))
        d *= 2
    return x


def _mlstm_kernel(qraw_ref, kraw_ref, v_ref, og_ref, gr_ref, cw_ref, cb_ref, ng_ref,
                  y_ref,
                  ks_ref, qt_ref, kt_ref, vt_ref, rcol_ref, b_ref, dm_ref, w_ref, r_ref, am_ref, bt_ref,
                  cst_ref, mst_ref, cin_ref, minf_ref, minb_ref, hs_ref, *, seq):
    L = MLSTM_CHUNK
    H = MLSTM_HEADS
    dh = MLSTM_HEAD_DIM
    nc = seq // L
    width = MLSTM_WIDTH
    halo = 16
    row8 = lax.broadcasted_iota(jnp.int32, (8, L), 0)

    def head(h):
        return slice(h * dh, (h + 1) * dh)

    assert nc * 8 == V7X_LANES and L == V7X_LANES
    shape = (nc * 8, L)
    gi = gr_ref[:, 0:8, :].reshape(shape)
    lf = _log_sigmoid(gr_ref[:, 8:16, :].reshape(shape))
    fwd = (lax.broadcasted_iota(jnp.int32, shape, 0) & 7) < H
    lane = lax.broadcasted_iota(jnp.int32, shape, 1)
    b = jnp.where(fwd, _scan(lf, 1, False, jnp.add, 0.0), _scan(lf, 1, True, jnp.add, 0.0))
    b_tot = jnp.sum(jnp.where(lane == jnp.where(fwd, L - 1, 0), b, 0.0), axis=1, keepdims=True)
    a = b_tot - b + gi
    a_max = jnp.max(a, axis=1, keepdims=True)
    r = gi - b
    d_max = b + jnp.where(fwd, _scan(r, 1, False, jnp.maximum, NEG_BIG),
                          _scan(r, 1, True, jnp.maximum, NEG_BIG))
    r_ref[...] = (r * LOG2E).reshape(nc, 8, L)
    b_ref[...] = (b * LOG2E).reshape(nc, 8, L)
    dm_ref[...] = (d_max * LOG2E).reshape(nc, 8, L)
    w_ref[...] = jnp.exp(a - a_max).reshape(nc, 8, L)
    am_ref[...] = jnp.broadcast_to(a_max, shape).reshape(nc, 8, L)
    bt_ref[...] = jnp.broadcast_to(b_tot, shape).reshape(nc, 8, L)

    def prep(c, carry):
        r0 = pl.multiple_of(c * L, L)
        rows = pl.ds(r0, L)
        rid = lax.broadcasted_iota(jnp.int32, (L, width), 0)
        has_prev = jnp.where(c > 0, 1.0, 0.0)
        has_next = jnp.where(c < nc - 1, 1.0, 0.0)
        prev_at = pl.multiple_of(jnp.maximum(r0 - halo, 0), halo)
        next_at = pl.multiple_of(jnp.minimum(r0 + L, seq - halo), halo)
        acts = []
        for idx, raw_ref in enumerate((qraw_ref, kraw_ref)):
            cols = slice(idx * width, (idx + 1) * width)
            cur = raw_ref[rows, :].astype(F32)
            prev_row = raw_ref[pl.ds(prev_at, halo), :].astype(F32)[halo - 1:halo, :] * has_prev
            next_row = raw_ref[pl.ds(next_at, halo), :].astype(F32)[0:1, :] * has_next
            before = jnp.where(rid == 0, prev_row, pltpu.roll(cur, 1, 0))
            after = jnp.where(rid == L - 1, next_row, pltpu.roll(cur, L - 1, 0))
            u = (before * cw_ref[0:1, cols] + cur * cw_ref[1:2, cols] + after * cw_ref[2:3, cols]
                 + cb_ref[:, cols])
            u = u / (1.0 + jnp.exp2(u * (-LOG2E)))
            if idx == 1:
                u = u * (dh ** -0.5)
            acts.append(u)
        q_act, k_act = acts
        ks_ref[rows, :] = k_act.astype(BF16)
        ones_rows = jnp.ones((halo, L), BF16)
        for h in range(H):
            qt_ref[c, h] = q_act[:, head(h)].T.astype(BF16)
            kt_ref[c, h] = k_act[:, head(h)].T.astype(BF16)
            vt_ref[c, h, 0:dh, :] = v_ref[rows, head(h)].astype(F32).T.astype(BF16)
            vt_ref[c, h, dh:dh + halo, :] = ones_rows
        rcol_ref[rows, :] = jnp.concatenate([r_ref[c], jnp.zeros((L - 8, L), F32)], axis=0).T
        return carry

    lax.fori_loop(0, nc, prep, 0, unroll=2)

    cst_ref[...] = jnp.zeros_like(cst_ref)
    mst_ref[...] = jnp.zeros_like(mst_ref)

    def scan(i, carry):
        c_f = i
        c_b = nc - 1 - i
        fwd = row8 < H
        bt = jnp.where(fwd, bt_ref[c_f], bt_ref[c_b])
        am = jnp.where(fwd, am_ref[c_f], am_ref[c_b])
        m_old = mst_ref[...]
        m_new = jnp.maximum(bt + m_old, am)
        s_prev = jnp.exp(bt + m_old - m_new)
        s_loc = jnp.exp(am - m_new)
        minf_ref[c_f] = m_old
        minb_ref[c_b] = m_old
        mst_ref[...] = m_new
        for d, c in enumerate((c_f, c_b)):
            wr = w_ref[c]
            for h in range(H):
                j = d * H + h
                kw_t = (kt_ref[c, h].astype(F32) * wr[j:j + 1, :]).astype(BF16)
                c_loc = _dot_nt(vt_ref[c, h], kw_t)
                c_old = cst_ref[j]
                cin_ref[c, j] = c_old.astype(BF16)
                cst_ref[j] = s_prev[j:j + 1, :] * c_old + s_loc[j:j + 1, :] * c_loc
        return carry

    lax.fori_loop(0, nc, scan, 0, unroll=4)

    def emit(c):
        rows = pl.ds(pl.multiple_of(c * L, L), L)
        m_in = jnp.where(row8 < H, minf_ref[c], minb_ref[c]) * LOG2E
        bc = b_ref[c]
        m_t = jnp.maximum(bc + m_in, dm_ref[c])
        b_rel = bc - m_t
        scale_in = jnp.exp2(b_rel + m_in)
        floor = jnp.exp2(-m_t)
        rcol = rcol_ref[rows, :]
        s_id = lax.broadcasted_iota(jnp.int32, (L, L), 0)
        t_id = lax.broadcasted_iota(jnp.int32, (L, L), 1)
        masks = (s_id <= t_id, s_id >= t_id)
        for h in range(H):
            q_t = qt_ref[c, h]
            s_t = _dot(ks_ref[rows, head(h)], q_t)
            weights, sums, inters = [], [], []
            for d in range(2):
                j = d * H + h
                logw = jnp.broadcast_to(rcol[:, j:j + 1], (L, L)) + b_rel[j:j + 1, :]
                sc = s_t * jnp.exp2(jnp.where(masks[d], logw, NEG_BIG))
                sums.append(jnp.sum(sc, axis=0, keepdims=True))
                weights.append(sc.astype(BF16))
                inters.append(_dot(cin_ref[c, j], q_t))
            num = _dot(vt_ref[c, h, 0:dh, :], jnp.concatenate(weights, axis=1))
            hsum = None
            for d in range(2):
                j = d * H + h
                den = sums[d] + scale_in[j:j + 1, :] * inters[d][dh:dh + 1, :]
                inv = 1.0 / jnp.maximum(jnp.abs(den), floor[j:j + 1, :])
                hd = num[:, d * L:(d + 1) * L] * inv + inters[d][0:dh, :] * (scale_in[j:j + 1, :] * inv)
                hsum = hd if hsum is None else hsum + hd
            hs_ref[c, h] = hsum

    def finish(c):
        rows = pl.ds(pl.multiple_of(c * L, L), L)
        for h in range(H):
            hsum = hs_ref[c, h]
            xc = hsum - jnp.mean(hsum, axis=0, keepdims=True)
            var = jnp.mean(xc * xc, axis=0, keepdims=True)
            y = (xc * lax.rsqrt(var + NORM_EPS)).T * ng_ref[:, head(h)] * og_ref[rows, head(h)].astype(F32)
            y_ref[rows, head(h)] = y.astype(BF16)

    def emit_and_finish(c, carry):
        finish(c - 1)
        emit(c)
        return carry

    emit(0)
    lax.fori_loop(1, nc, emit_and_finish, 0, unroll=3)
    finish(nc - 1)


def _mlstm(qraw, kraw, v, og, grow, conv_w, conv_b, norm_g, batch, seq):
    t = qraw.shape[0]
    nc = seq // MLSTM_CHUNK
    nd = 2 * MLSTM_HEADS
    by_batch = lambda w: pl.BlockSpec((seq, w), lambda b: (b, 0))
    dh = MLSTM_HEAD_DIM
    aug = dh + 16
    tile = (nc, MLSTM_HEADS, dh, MLSTM_CHUNK)
    rowvec = pltpu.VMEM((nc, 8, MLSTM_CHUNK), F32)
    scratch = [
        pltpu.VMEM((seq, MLSTM_WIDTH), BF16),
        pltpu.VMEM(tile, BF16),
        pltpu.VMEM(tile, BF16),
        pltpu.VMEM((nc, MLSTM_HEADS, aug, MLSTM_CHUNK), BF16),
        pltpu.VMEM((seq, V7X_LANES), F32),
        rowvec,
        rowvec,
        rowvec,
        rowvec,
        rowvec,
        rowvec,
        pltpu.VMEM((nd, aug, dh), F32),
        pltpu.VMEM((8, MLSTM_CHUNK), F32),
        pltpu.VMEM((nc, nd, aug, dh), BF16),
        rowvec,
        rowvec,
        pltpu.VMEM(tile, F32),
    ]
    return pl.pallas_call(
        functools.partial(_mlstm_kernel, seq=seq),
        out_shape=jax.ShapeDtypeStruct((t, MLSTM_WIDTH), BF16),
        grid=(batch,),
        in_specs=[by_batch(MLSTM_WIDTH)] * 4 + [
            pl.BlockSpec((nc, 16, MLSTM_CHUNK), lambda b: (b, 0, 0)),
            _resident(conv_w.shape), _resident(conv_b.shape), _resident(norm_g.shape)],
        out_specs=by_batch(MLSTM_WIDTH),
        scratch_shapes=scratch,
        compiler_params=_params(1),
        name="bidir_mlstm",
    )(qraw, kraw, v, og, grow, conv_w, conv_b, norm_g)


def _merge_kernel(x_ref, ya_ref, ym_ref, ga_ref, gm_ref, wa_ref, wb_ref, wo_ref, o_ref):
    merged = (ga_ref[...].astype(F32) * _dot(ya_ref[...], wa_ref[...])
              + gm_ref[...].astype(F32) * _dot(ym_ref[...], wb_ref[...]))
    o_ref[...] = x_ref[...] + _dot(merged.astype(BF16), wo_ref[...])


def _merge(x, ya, ym, ga, gm, wa, wb, wo, layer):
    t, d = x.shape
    tm = MERGE_ROWS
    return pl.pallas_call(
        _merge_kernel,
        out_shape=jax.ShapeDtypeStruct((t, d), F32),
        grid=(t // tm,),
        in_specs=[_rows(tm, d), _rows(tm, ATT_WIDTH), _rows(tm, MLSTM_WIDTH), _rows(tm, d), _rows(tm, d),
                  _layer(wa.shape, layer), _layer(wb.shape, layer), _layer(wo.shape, layer)],
        out_specs=_rows(tm, d),
        compiler_params=_params(1),
        name="merge_out",
    )(x, ya, ym, ga, gm, wa, wb, wo)


def _w_in_offsets():
    o_v = ATT_WIDTH + 2 * ATT_KV_WIDTH
    o_km = o_v + 2 * MLSTM_WIDTH
    o_om = o_km + 2 * MLSTM_WIDTH
    o_g = o_om + 4 * MLSTM_HEADS
    return o_v, o_km, o_om, o_g


def _split_kernel(w_ref, qkv_ref, qkm_ref, vo_ref, gate_ref, mg_ref):
    o_v, o_km, o_om, o_g = _w_in_offsets()
    qkv_ref[0] = w_ref[0, :, :o_v].astype(BF16)
    qkm_ref[0] = w_ref[0, :, o_v:o_km].astype(BF16)
    vo_ref[0] = w_ref[0, :, o_km:o_om].astype(BF16)
    gate_ref[0] = w_ref[0, :, o_om:o_g]
    mg_ref[0] = w_ref[0, :, o_g:].astype(BF16)


def _split_w_in(w_in):
    depth, rows, cols = w_in.shape
    o_v, o_km, o_om, o_g = _w_in_offsets()
    widths = (o_v, o_km - o_v, o_om - o_km, o_g - o_om, cols - o_g)
    dtypes = (BF16, BF16, BF16, F32, BF16)
    spec = lambda w: pl.BlockSpec((1, CAST_ROWS, w), lambda l, i: (l, i, 0))
    return pl.pallas_call(
        _split_kernel,
        out_shape=[jax.ShapeDtypeStruct((depth, rows, w), dt) for w, dt in zip(widths, dtypes)],
        grid=(depth, rows // CAST_ROWS),
        in_specs=[spec(cols)],
        out_specs=[spec(w) for w in widths],
        compiler_params=_params(2),
        name="split_w_in",
    )(w_in)


def _proj_weights(wg, gate_bias, q_gain, k_gain):
    H = MLSTM_HEADS
    order_i = jnp.concatenate([wg[:, 0:H], wg[:, 2 * H:3 * H]], axis=1)
    order_f = jnp.concatenate([wg[:, H:2 * H], wg[:, 3 * H:4 * H]], axis=1)
    gate_row = jnp.concatenate([order_i, order_f], axis=1).T
    b_i = jnp.concatenate([gate_bias[0:H], gate_bias[2 * H:3 * H]])
    b_f = jnp.concatenate([gate_bias[H:2 * H], gate_bias[3 * H:4 * H]])
    return {
        "gate_row": gate_row.astype(BF16),
        "bias_row": jnp.concatenate([b_i, b_f]).reshape(4 * H, 1).astype(F32),
        "q_gain": jnp.tile(q_gain, ATT_HEADS).reshape(1, ATT_WIDTH).astype(F32),
        "k_gain": jnp.tile(k_gain, ATT_KV_HEADS).reshape(1, ATT_KV_WIDTH).astype(F32),
    }


def kernel(x, positions, ffn1_norm, ffn1_w_gate, ffn1_w_up, ffn1_w_down, mix_norm, w_in, mlstm_gate_bias, attn_q_norm, attn_k_norm, attn_sink, mlstm_conv_w, mlstm_conv_b, mlstm_out_norm, w_branch_attn, w_branch_mlstm, w_out, ffn2_norm, ffn2_w_gate, ffn2_w_up, ffn2_w_down, block_out_norm):
    batch, seq, d = x.shape
    depth = w_in.shape[0]
    t = batch * seq
    xt = x.reshape(t, d)
    cos_t, sin_t = _rope_tables(positions)
    wg1, wu1, wg2, wu2 = _to_bf16(ffn1_w_gate, ffn1_w_up, ffn2_w_gate, ffn2_w_up)
    wd1, wd2 = _to_bf16(ffn1_w_down, ffn2_w_down)
    wa, wb = _to_bf16(w_branch_attn, w_branch_mlstm)
    (wo,) = _to_bf16(w_out)
    w_qkv, w_qkm, w_vo, w_gate, w_mg = _split_w_in(w_in)
    big = {"qkv": w_qkv, "qkm": w_qkm, "vo": w_vo, "merge": w_mg}
    for l in range(depth):
        xt = _ffn(xt, ffn1_norm[l], wg1, wu1, wd1, l)
        pw = _proj_weights(w_gate[l], mlstm_gate_bias[l], attn_q_norm[l], attn_k_norm[l])
        qa, k4, v4, qm, km, vm, og, grow, ga, gm = _proj(xt, mix_norm[l], big, l, pw, cos_t, sin_t)
        ya = _attention(qa, k4, v4, attn_sink[l].astype(F32), batch, seq)
        ym = _mlstm(qm, km, vm, og, grow, mlstm_conv_w[l].astype(F32),
                    mlstm_conv_b[l].reshape(1, -1).astype(F32),
                    mlstm_out_norm[l].reshape(1, -1).astype(F32), batch, seq)
        xt = _merge(xt, ya, ym, ga, gm, wa, wb, wo, l)
        xt = _ffn(xt, ffn2_norm[l], wg2, wu2, wd2, l, final_gain=block_out_norm[l])
    return xt.reshape(batch, seq, d)
```

```python
import functools

import numpy as np
import jax
import jax.numpy as jnp
from jax import lax
from jax.experimental import pallas as pl
from jax.experimental.pallas import tpu as pltpu

F32 = jnp.float32
BF16 = jnp.bfloat16

ATT_HEAD_DIM = 64
ATT_HEADS = 8
ATT_KV_HEADS = 2
ATT_WIDTH = ATT_HEADS * ATT_HEAD_DIM
ATT_KV_WIDTH = ATT_KV_HEADS * ATT_HEAD_DIM
WINDOW = 128
ATT_BLOCK = 128
ROPE_DIM = ATT_HEAD_DIM // 4
ROPE_THETA = 500000.0
MLSTM_HEADS = 4
MLSTM_HEAD_DIM = 128
MLSTM_WIDTH = MLSTM_HEADS * MLSTM_HEAD_DIM
MLSTM_CHUNK = 128
NORM_EPS = 1e-6
NEG_BIG = -1e30
LOG2E = 1.4426950408889634

V7X_LANES = 128
V7X_MXU_COLS = 256
V7X_VMEM_LIMIT_BYTES = 56 * 1024 * 1024

FFN_ROWS = 1024
PROJ_ROWS = 512
MERGE_ROWS = 1024
CAST_ROWS = 256
ROPE_ROWS = 2048


def _params(n_axes):
    return pltpu.CompilerParams(
        dimension_semantics=("parallel",) * n_axes,
        vmem_limit_bytes=V7X_VMEM_LIMIT_BYTES,
    )


def _resident(shape):
    nd = len(shape)
    return pl.BlockSpec(shape, lambda *_: (0,) * nd, pipeline_mode=pl.Buffered(1))


def _layer(shape, layer):
    return pl.BlockSpec((None,) + tuple(shape[1:]), lambda *_: (layer, 0, 0), pipeline_mode=pl.Buffered(1))


def _rows(tm, width):
    return pl.BlockSpec((tm, width), lambda i: (i, 0))


def _cast_kernel(*refs):
    n = len(refs) // 2
    for src, dst in zip(refs[:n], refs[n:]):
        dst[...] = src[...].astype(BF16)


def _to_bf16(*ws):
    depth, rows, cols = ws[0].shape
    assert all(w.shape == ws[0].shape for w in ws) and rows % CAST_ROWS == 0
    spec = pl.BlockSpec((1, CAST_ROWS, cols), lambda l, i: (l, i, 0))
    return pl.pallas_call(
        _cast_kernel,
        out_shape=[jax.ShapeDtypeStruct(w.shape, BF16) for w in ws],
        grid=(depth, rows // CAST_ROWS),
        in_specs=[spec] * len(ws),
        out_specs=[spec] * len(ws),
        compiler_params=_params(2),
        name="cast_bf16",
    )(*ws)


def _rms(x, gain):
    ms = jnp.mean(x * x, axis=-1, keepdims=True)
    return x * lax.rsqrt(ms + NORM_EPS) * gain


def _dot(a, b):
    return jnp.dot(a, b, preferred_element_type=F32)


def _dot_nt(a, b):
    return lax.dot_general(a, b, (((1,), (1,)), ((), ())), preferred_element_type=F32)


def _dot_tn(a, b):
    return lax.dot_general(a, b, (((0,), (0,)), ((), ())), preferred_element_type=F32)


def _sigmoid(x):
    return 0.5 * jnp.tanh(0.5 * x) + 0.5


def _log_sigmoid(x):
    return jnp.minimum(x, 0.0) - jnp.log1p(jnp.exp(-jnp.abs(x)))


def _rope_kernel(pos_ref, freq_ref, cos_ref, sin_ref):
    ang = pos_ref[...].astype(F32) * freq_ref[...]
    j = lax.broadcasted_iota(jnp.int32, ang.shape, 1) % ATT_HEAD_DIM
    half = ROPE_DIM // 2
    c = jnp.cos(ang)
    s = jnp.sin(ang)
    cos_ref[...] = jnp.where(j < ROPE_DIM, c, 1.0)
    sin_ref[...] = jnp.where(j < half, -s, jnp.where(j < ROPE_DIM, s, 0.0))


def _rope_tables(positions):
    t = positions.size
    half = ROPE_DIM // 2
    inv_freq = np.power(np.float32(ROPE_THETA),
                        -np.arange(half, dtype=np.float32) * np.float32(2.0 / ROPE_DIM)).astype(np.float32)
    lane = np.arange(V7X_LANES) % ATT_HEAD_DIM
    freq = np.where(lane < ROPE_DIM, inv_freq[lane % half], 0.0).astype(np.float32)[None, :]
    pos = positions.reshape(t, 1)
    return pl.pallas_call(
        _rope_kernel,
        out_shape=[jax.ShapeDtypeStruct((t, V7X_LANES), F32)] * 2,
        grid=(t // ROPE_ROWS,),
        in_specs=[_rows(ROPE_ROWS, 1), _resident((1, V7X_LANES))],
        out_specs=[_rows(ROPE_ROWS, V7X_LANES)] * 2,
        compiler_params=_params(1),
        name="rope_tables",
    )(pos, jnp.asarray(freq))


def _ffn_kernel(*refs, d_ff, final_norm):
    if final_norm:
        x_ref, g_ref, wg_ref, wu_ref, wd_ref, go_ref, o_ref, hn_ref, a_ref = refs
    else:
        x_ref, g_ref, wg_ref, wu_ref, wd_ref, o_ref, hn_ref, a_ref = refs
    hn_ref[...] = _rms(x_ref[...], g_ref[...]).astype(BF16)
    for c0 in range(0, d_ff, V7X_MXU_COLS):
        cols = slice(c0, c0 + V7X_MXU_COLS)
        hn = hn_ref[...]
        gate = _dot(hn, wg_ref[:, cols])
        up = _dot(hn, wu_ref[:, cols])
        a_ref[:, cols] = (gate * _sigmoid(gate) * up).astype(BF16)
    out = x_ref[...] + 0.5 * _dot(a_ref[...], wd_ref[...])
    if final_norm:
        out = _rms(out, go_ref[...])
    o_ref[...] = out


def _ffn(x, gain, wg, wu, wd, layer, final_gain=None):
    t, d = x.shape
    d_ff = wg.shape[2]
    assert d_ff % V7X_MXU_COLS == 0 and t % FFN_ROWS == 0
    final_norm = final_gain is not None
    args = [x, gain.reshape(1, d), wg, wu, wd]
    in_specs = [_rows(FFN_ROWS, d), _resident((1, d)), _layer(wg.shape, layer),
                _layer(wu.shape, layer), _layer(wd.shape, layer)]
    if final_norm:
        args.append(final_gain.reshape(1, d))
        in_specs.append(_resident((1, d)))
    return pl.pallas_call(
        functools.partial(_ffn_kernel, d_ff=d_ff, final_norm=final_norm),
        out_shape=jax.ShapeDtypeStruct((t, d), F32),
        grid=(t // FFN_ROWS,),
        in_specs=in_specs,
        out_specs=_rows(FFN_ROWS, d),
        scratch_shapes=[pltpu.VMEM((FFN_ROWS, d), BF16), pltpu.VMEM((FFN_ROWS, d_ff), BF16)],
        compiler_params=_params(1),
        name="ffn_final" if final_norm else "ffn",
    )(*args)


def _head_rms(x, ones_bd, gain):
    ss = _dot((x * x).astype(BF16), ones_bd)
    return x * lax.rsqrt(ss * (1.0 / ATT_HEAD_DIM) + NORM_EPS) * gain


def _rope(x, cos, sin):
    n = x.shape[1]
    half = ROPE_DIM // 2
    j = lax.broadcasted_iota(jnp.int32, x.shape, 1) % ATT_HEAD_DIM
    partner = jnp.where(j < half, pltpu.roll(x, n - half, 1), pltpu.roll(x, half, 1))
    return x * cos + partner * sin


def _lo_hi(x):
    lo = lax.broadcasted_iota(jnp.int32, x.shape, 1) < ATT_HEAD_DIM
    xr = pltpu.roll(x, ATT_HEAD_DIM, 1)
    zero = jnp.zeros_like(x)
    return jnp.concatenate([jnp.where(lo, x, zero), jnp.where(lo, zero, xr),
                            jnp.where(lo, xr, zero), jnp.where(lo, zero, x)], axis=1)


def _proj_kernel(x_ref, g_ref, wqkv_ref, wqkm_ref, wvo_ref, wgr_ref, bgr_ref,
                 wmg_ref, qg_ref, kg_ref, cos_ref, sin_ref, ones_ref,
                 qa_ref, k4_ref, v4_ref, qm_ref, km_ref, vm_ref, og_ref, gr_ref,
                 ga_ref, gmm_ref):
    hn = _rms(x_ref[...], g_ref[...]).astype(BF16)

    def proj(w):
        return _dot(hn, w)

    cos = cos_ref[...]
    sin = sin_ref[...]
    ones_bd = ones_ref[...]

    q = proj(wqkv_ref[:, :ATT_WIDTH])
    k = proj(wqkv_ref[:, ATT_WIDTH:ATT_WIDTH + ATT_KV_WIDTH])
    v = proj(wqkv_ref[:, ATT_WIDTH + ATT_KV_WIDTH:])
    qm_ref[...] = proj(wqkm_ref[:, :MLSTM_WIDTH]).astype(BF16)
    q = _head_rms(q, ones_bd, qg_ref[...])
    km_ref[...] = proj(wqkm_ref[:, MLSTM_WIDTH:]).astype(BF16)
    k = _head_rms(k, ones_bd[:ATT_KV_WIDTH, :ATT_KV_WIDTH], kg_ref[...])
    vm_ref[...] = proj(wvo_ref[:, :MLSTM_WIDTH]).astype(BF16)
    og_ref[...] = _sigmoid(proj(wvo_ref[:, MLSTM_WIDTH:])).astype(BF16)
    gr = _dot_nt(wgr_ref[...], hn) + bgr_ref[...]
    for c in range(gr_ref.shape[0]):
        gr_ref[c] = gr[:, c * MLSTM_CHUNK:(c + 1) * MLSTM_CHUNK]

    reps = ATT_WIDTH // V7X_LANES
    q = _rope(q, jnp.concatenate([cos] * reps, axis=1), jnp.concatenate([sin] * reps, axis=1))
    qa_ref[...] = (q * (ATT_HEAD_DIM ** -0.5 * LOG2E)).astype(BF16)
    k4_ref[...] = _lo_hi(_rope(k, cos, sin)).astype(BF16)
    v4_ref[...] = _lo_hi(v).astype(BF16)

    d = ga_ref.shape[1]
    piece = 2 * V7X_MXU_COLS
    for c0 in range(0, d, piece):
        ga_ref[:, c0:c0 + piece] = _sigmoid(proj(wmg_ref[:, c0:c0 + piece])).astype(BF16)
    for c0 in range(0, d, piece):
        gmm_ref[:, c0:c0 + piece] = _sigmoid(proj(wmg_ref[:, d + c0:d + c0 + piece])).astype(BF16)


def _proj(x, gain, big, layer, w, cos_t, sin_t):
    t, d = x.shape
    tm = PROJ_ROWS
    nck = tm // MLSTM_CHUNK
    heads = np.arange(ATT_WIDTH) // ATT_HEAD_DIM
    ones_bd = jnp.asarray((heads[:, None] == heads[None, :]).astype(np.float32), dtype=BF16)
    out_shape = [
        jax.ShapeDtypeStruct((t, ATT_WIDTH), BF16),
        jax.ShapeDtypeStruct((t, 4 * V7X_LANES), BF16),
        jax.ShapeDtypeStruct((t, 4 * V7X_LANES), BF16),
        jax.ShapeDtypeStruct((t, MLSTM_WIDTH), BF16),
        jax.ShapeDtypeStruct((t, MLSTM_WIDTH), BF16),
        jax.ShapeDtypeStruct((t, MLSTM_WIDTH), BF16),
        jax.ShapeDtypeStruct((t, MLSTM_WIDTH), BF16),
        jax.ShapeDtypeStruct((t // MLSTM_CHUNK, 16, MLSTM_CHUNK), F32),
        jax.ShapeDtypeStruct((t, d), BF16),
        jax.ShapeDtypeStruct((t, d), BF16),
    ]
    out_specs = [
        _rows(tm, ATT_WIDTH), _rows(tm, 4 * V7X_LANES), _rows(tm, 4 * V7X_LANES),
        _rows(tm, MLSTM_WIDTH), _rows(tm, MLSTM_WIDTH), _rows(tm, MLSTM_WIDTH), _rows(tm, MLSTM_WIDTH),
        pl.BlockSpec((nck, 16, MLSTM_CHUNK), lambda i: (i, 0, 0)),
        _rows(tm, d), _rows(tm, d),
    ]
    args = [x, gain.reshape(1, d), big["qkv"], big["qkm"], big["vo"], w["gate_row"],
            w["bias_row"], big["merge"], w["q_gain"], w["k_gain"], cos_t, sin_t, ones_bd]
    in_specs = [_rows(tm, d)] + [
        _layer(a.shape, layer) if a.ndim == 3 else _resident(a.shape) for a in args[1:10]] + [
        _rows(tm, V7X_LANES), _rows(tm, V7X_LANES), _resident(ones_bd.shape)]
    return pl.pallas_call(
        _proj_kernel,
        out_shape=out_shape,
        grid=(t // tm,),
        in_specs=in_specs,
        out_specs=out_specs,
        compiler_params=_params(1),
        name="mixer_proj",
    )(*args)


def _attn_kernel(sink_ref, q_ref, k4_ref, v4_ref, o_ref, *, seq):
    blk = ATT_BLOCK
    assert WINDOW == blk
    nb = seq // blk
    group = ATT_HEADS // ATT_KV_HEADS

    def body(n, carry):
        r0 = pl.multiple_of(n * blk, blk)
        left = pl.multiple_of(jnp.maximum(r0 - blk, 0), blk)
        right = pl.multiple_of(jnp.minimum(r0 + blk, seq - blk), blk)
        qb = q_ref[pl.ds(r0, blk), :]
        kb = jnp.concatenate([k4_ref[pl.ds(left, blk), :], k4_ref[pl.ds(r0, blk), :],
                              k4_ref[pl.ds(right, blk), :]], axis=0)
        vb = jnp.concatenate([v4_ref[pl.ds(left, blk), :], v4_ref[pl.ds(r0, blk), :],
                              v4_ref[pl.ds(right, blk), :]], axis=0)
        ql = lax.broadcasted_iota(jnp.int32, (blk, blk), 0)
        kl = lax.broadcasted_iota(jnp.int32, (blk, blk), 1)
        left_ok = kl >= ql + jnp.where(n > 0, 0, blk)
        right_ok = kl + jnp.where(n < nb - 1, 0, blk) <= ql
        lo = lax.broadcasted_iota(jnp.int32, (blk, V7X_LANES), 1) < ATT_HEAD_DIM
        lo_band = lax.broadcasted_iota(jnp.int32, (3 * blk, V7X_LANES), 1) < ATT_HEAD_DIM
        ones_lo = jnp.where(lo_band, 1.0, 0.0).astype(BF16)
        ones_hi = jnp.where(lo_band, 0.0, 1.0).astype(BF16)
        outs = []
        for g in range(ATT_KV_HEADS):
            kv_lo = slice(2 * g * V7X_LANES, (2 * g + 1) * V7X_LANES)
            kv_hi = slice((2 * g + 1) * V7X_LANES, (2 * g + 2) * V7X_LANES)
            v_aug = jnp.concatenate([jnp.concatenate([vb[:, kv_lo], ones_lo], axis=1),
                                     jnp.concatenate([vb[:, kv_hi], ones_hi], axis=1)], axis=0)
            k_both = jnp.concatenate([kb[:, kv_lo], kb[:, kv_hi]], axis=0)
            for p in range(group // 2):
                pair = g * (group // 2) + p
                qp = qb[:, pair * V7X_LANES:(pair + 1) * V7X_LANES]
                s_both = _dot_nt(qp, k_both)
                probs, shifts = [], []
                for half in range(2):
                    sink = sink_ref[2 * pair + half] * LOG2E
                    s = s_both[:, half * 3 * blk:(half + 1) * 3 * blk]
                    s_l = jnp.where(left_ok, s[:, :blk], NEG_BIG)
                    s_m = s[:, blk:2 * blk]
                    s_r = jnp.where(right_ok, s[:, 2 * blk:], NEG_BIG)
                    m = jnp.max(jnp.maximum(jnp.maximum(s_l, s_r), s_m), axis=-1, keepdims=True)
                    m = jnp.maximum(m, sink)
                    probs += [jnp.exp2(s_l - m), jnp.exp2(s_m - m), jnp.exp2(s_r - m)]
                    shifts.append(sink - m)
                acc = _dot(jnp.concatenate(probs, axis=1).astype(BF16), v_aug)
                den = acc[:, V7X_LANES:] + jnp.exp2(jnp.where(lo, shifts[0], shifts[1]))
                outs.append(acc[:, :V7X_LANES] / den)
        o_ref[pl.ds(r0, blk), :] = jnp.concatenate(outs, axis=1).astype(BF16)
        return carry

    lax.fori_loop(0, nb, body, 0, unroll=8)


def _attention(q, k4, v4, sink, batch, seq):
    t = q.shape[0]
    by_batch = lambda w: pl.BlockSpec((seq, w), lambda b: (b, 0))
    return pl.pallas_call(
        functools.partial(_attn_kernel, seq=seq),
        out_shape=jax.ShapeDtypeStruct((t, ATT_WIDTH), BF16),
        grid=(batch,),
        in_specs=[pl.BlockSpec(memory_space=pltpu.SMEM), by_batch(ATT_WIDTH),
                  by_batch(4 * V7X_LANES), by_batch(4 * V7X_LANES)],
        out_specs=by_batch(ATT_WIDTH),
        compiler_params=_params(1),
        name="window_attn",
    )(sink, q, k4, v4)


def _scan(x, axis, reverse, op, identity):
    n = x.shape[axis]
    idx = lax.broadcasted_iota(jnp.int32, x.shape, axis)
    d = 1
    while d < n:
        if reverse:
            x = op(x, jnp.where(idx < n - d, pltpu.roll(x, n - d, axis), identity))
        else:
            x = op(x, jnp.where(idx >= d, pltpu.roll(x, d, axis), identity))
        d *= 2
    return x


def _mlstm_kernel(qraw_ref, kraw_ref, v_ref, og_ref, gr_ref, cw_ref, cb_ref, ng_ref,
                  y_ref,
                  ks_ref, qt_ref, kt_ref, vt_ref, rcol_ref, b_ref, dm_ref, w_ref, r_ref, am_ref, bt_ref,
                  cst_ref, mst_ref, cin_ref, minf_ref, minb_ref, hs_ref, *, seq):
    L = MLSTM_CHUNK
    H = MLSTM_HEADS
    dh = MLSTM_HEAD_DIM
    nc = seq // L
    width = MLSTM_WIDTH
    halo = 16
    row8 = lax.broadcasted_iota(jnp.int32, (8, L), 0)

    def head(h):
        return slice(h * dh, (h + 1) * dh)

    assert nc * 8 == V7X_LANES and L == V7X_LANES
    shape = (nc * 8, L)
    gi = gr_ref[:, 0:8, :].reshape(shape)
    lf = _log_sigmoid(gr_ref[:, 8:16, :].reshape(shape))
    fwd = (lax.broadcasted_iota(jnp.int32, shape, 0) & 7) < H
    lane = lax.broadcasted_iota(jnp.int32, shape, 1)
    b = jnp.where(fwd, _scan(lf, 1, False, jnp.add, 0.0), _scan(lf, 1, True, jnp.add, 0.0))
    b_tot = jnp.sum(jnp.where(lane == jnp.where(fwd, L - 1, 0), b, 0.0), axis=1, keepdims=True)
    a = b_tot - b + gi
    a_max = jnp.max(a, axis=1, keepdims=True)
    r = gi - b
    d_max = b + jnp.where(fwd, _scan(r, 1, False, jnp.maximum, NEG_BIG),
                          _scan(r, 1, True, jnp.maximum, NEG_BIG))
    r_ref[...] = (r * LOG2E).reshape(nc, 8, L)
    b_ref[...] = (b * LOG2E).reshape(nc, 8, L)
    dm_ref[...] = (d_max * LOG2E).reshape(nc, 8, L)
    w_ref[...] = jnp.exp(a - a_max).reshape(nc, 8, L)
    am_ref[...] = jnp.broadcast_to(a_max, shape).reshape(nc, 8, L)
    bt_ref[...] = jnp.broadcast_to(b_tot, shape).reshape(nc, 8, L)

    def prep(c, carry):
        r0 = pl.multiple_of(c * L, L)
        rows = pl.ds(r0, L)
        rid = lax.broadcasted_iota(jnp.int32, (L, width), 0)
        has_prev = jnp.where(c > 0, 1.0, 0.0)
        has_next = jnp.where(c < nc - 1, 1.0, 0.0)
        prev_at = pl.multiple_of(jnp.maximum(r0 - halo, 0), halo)
        next_at = pl.multiple_of(jnp.minimum(r0 + L, seq - halo), halo)
        acts = []
        for idx, raw_ref in enumerate((qraw_ref, kraw_ref)):
            cols = slice(idx * width, (idx + 1) * width)
            cur = raw_ref[rows, :].astype(F32)
            prev_row = raw_ref[pl.ds(prev_at, halo), :].astype(F32)[halo - 1:halo, :] * has_prev
            next_row = raw_ref[pl.ds(next_at, halo), :].astype(F32)[0:1, :] * has_next
            before = jnp.where(rid == 0, prev_row, pltpu.roll(cur, 1, 0))
            after = jnp.where(rid == L - 1, next_row, pltpu.roll(cur, L - 1, 0))
            u = (before * cw_ref[0:1, cols] + cur * cw_ref[1:2, cols] + after * cw_ref[2:3, cols]
                 + cb_ref[:, cols])
            u = u / (1.0 + jnp.exp2(u * (-LOG2E)))
            if idx == 1:
                u = u * (dh ** -0.5)
            acts.append(u)
        q_act, k_act = acts
        ks_ref[rows, :] = k_act.astype(BF16)
        ones_rows = jnp.ones((halo, L), BF16)
        for h in range(H):
            qt_ref[c, h] = q_act[:, head(h)].T.astype(BF16)
            kt_ref[c, h] = k_act[:, head(h)].T.astype(BF16)
            vt_ref[c, h, 0:dh, :] = v_ref[rows, head(h)].astype(F32).T.astype(BF16)
            vt_ref[c, h, dh:dh + halo, :] = ones_rows
        rcol_ref[rows, :] = jnp.concatenate([r_ref[c], jnp.zeros((L - 8, L), F32)], axis=0).T
        return carry

    lax.fori_loop(0, nc, prep, 0, unroll=2)

    cst_ref[...] = jnp.zeros_like(cst_ref)
    mst_ref[...] = jnp.zeros_like(mst_ref)

    def scan(i, carry):
        c_f = i
        c_b = nc - 1 - i
        fwd = row8 < H
        bt = jnp.where(fwd, bt_ref[c_f], bt_ref[c_b])
        am = jnp.where(fwd, am_ref[c_f], am_ref[c_b])
        m_old = mst_ref[...]
        m_new = jnp.maximum(bt + m_old, am)
        s_prev = jnp.exp(bt + m_old - m_new)
        s_loc = jnp.exp(am - m_new)
        minf_ref[c_f] = m_old
        minb_ref[c_b] = m_old
        mst_ref[...] = m_new
        for d, c in enumerate((c_f, c_b)):
            wr = w_ref[c]
            for h in range(H):
                j = d * H + h
                kw_t = (kt_ref[c, h].astype(F32) * wr[j:j + 1, :]).astype(BF16)
                c_loc = _dot_nt(vt_ref[c, h], kw_t)
                c_old = cst_ref[j]
                cin_ref[c, j] = c_old.astype(BF16)
                cst_ref[j] = s_prev[j:j + 1, :] * c_old + s_loc[j:j + 1, :] * c_loc
        return carry

    lax.fori_loop(0, nc, scan, 0, unroll=4)

    def emit(c):
        rows = pl.ds(pl.multiple_of(c * L, L), L)
        m_in = jnp.where(row8 < H, minf_ref[c], minb_ref[c]) * LOG2E
        bc = b_ref[c]
        m_t = jnp.maximum(bc + m_in, dm_ref[c])
        b_rel = bc - m_t
        scale_in = jnp.exp2(b_rel + m_in)
        floor = jnp.exp2(-m_t)
        rcol = rcol_ref[rows, :]
        s_id = lax.broadcasted_iota(jnp.int32, (L, L), 0)
        t_id = lax.broadcasted_iota(jnp.int32, (L, L), 1)
        masks = (s_id <= t_id, s_id >= t_id)
        for h in range(H):
            q_t = qt_ref[c, h]
            s_t = _dot(ks_ref[rows, head(h)], q_t)
            weights, sums, inters = [], [], []
            for d in range(2):
                j = d * H + h
                logw = jnp.broadcast_to(rcol[:, j:j + 1], (L, L)) + b_rel[j:j + 1, :]
                sc = s_t * jnp.exp2(jnp.where(masks[d], logw, NEG_BIG))
                sums.append(jnp.sum(sc, axis=0, keepdims=True))
                weights.append(sc.astype(BF16))
                inters.append(_dot(cin_ref[c, j], q_t))
            num = _dot(vt_ref[c, h, 0:dh, :], jnp.concatenate(weights, axis=1))
            hsum = None
            for d in range(2):
                j = d * H + h
                den = sums[d] + scale_in[j:j + 1, :] * inters[d][dh:dh + 1, :]
                inv = 1.0 / jnp.maximum(jnp.abs(den), floor[j:j + 1, :])
                hd = num[:, d * L:(d + 1) * L] * inv + inters[d][0:dh, :] * (scale_in[j:j + 1, :] * inv)
                hsum = hd if hsum is None else hsum + hd
            hs_ref[c, h] = hsum

    def finish(c):
        rows = pl.ds(pl.multiple_of(c * L, L), L)
        for h in range(H):
            hsum = hs_ref[c, h]
            xc = hsum - jnp.mean(hsum, axis=0, keepdims=True)
            var = jnp.mean(xc * xc, axis=0, keepdims=True)
            y = (xc * lax.rsqrt(var + NORM_EPS)).T * ng_ref[:, head(h)] * og_ref[rows, head(h)].astype(F32)
            y_ref[rows, head(h)] = y.astype(BF16)

    def emit_and_finish(c, carry):
        finish(c - 1)
        emit(c)
        return carry

    emit(0)
    lax.fori_loop(1, nc, emit_and_finish, 0, unroll=3)
    finish(nc - 1)


def _mlstm(qraw, kraw, v, og, grow, conv_w, conv_b, norm_g, batch, seq):
    t = qraw.shape[0]
    nc = seq // MLSTM_CHUNK
    nd = 2 * MLSTM_HEADS
    by_batch = lambda w: pl.BlockSpec((seq, w), lambda b: (b, 0))
    dh = MLSTM_HEAD_DIM
    aug = dh + 16
    tile = (nc, MLSTM_HEADS, dh, MLSTM_CHUNK)
    rowvec = pltpu.VMEM((nc, 8, MLSTM_CHUNK), F32)
    scratch = [
        pltpu.VMEM((seq, MLSTM_WIDTH), BF16),
        pltpu.VMEM(tile, BF16),
        pltpu.VMEM(tile, BF16),
        pltpu.VMEM((nc, MLSTM_HEADS, aug, MLSTM_CHUNK), BF16),
        pltpu.VMEM((seq, V7X_LANES), F32),
        rowvec,
        rowvec,
        rowvec,
        rowvec,
        rowvec,
        rowvec,
        pltpu.VMEM((nd, aug, dh), F32),
        pltpu.VMEM((8, MLSTM_CHUNK), F32),
        pltpu.VMEM((nc, nd, aug, dh), BF16),
        rowvec,
        rowvec,
        pltpu.VMEM(tile, F32),
    ]
    return pl.pallas_call(
        functools.partial(_mlstm_kernel, seq=seq),
        out_shape=jax.ShapeDtypeStruct((t, MLSTM_WIDTH), BF16),
        grid=(batch,),
        in_specs=[by_batch(MLSTM_WIDTH)] * 4 + [
            pl.BlockSpec((nc, 16, MLSTM_CHUNK), lambda b: (b, 0, 0)),
            _resident(conv_w.shape), _resident(conv_b.shape), _resident(norm_g.shape)],
        out_specs=by_batch(MLSTM_WIDTH),
        scratch_shapes=scratch,
        compiler_params=_params(1),
        name="bidir_mlstm",
    )(qraw, kraw, v, og, grow, conv_w, conv_b, norm_g)


def _merge_kernel(x_ref, ya_ref, ym_ref, ga_ref, gm_ref, wa_ref, wb_ref, wo_ref, o_ref):
    merged = (ga_ref[...].astype(F32) * _dot(ya_ref[...], wa_ref[...])
              + gm_ref[...].astype(F32) * _dot(ym_ref[...], wb_ref[...]))
    o_ref[...] = x_ref[...] + _dot(merged.astype(BF16), wo_ref[...])


def _merge(x, ya, ym, ga, gm, wa, wb, wo, layer):
    t, d = x.shape
    tm = MERGE_ROWS
    return pl.pallas_call(
        _merge_kernel,
        out_shape=jax.ShapeDtypeStruct((t, d), F32),
        grid=(t // tm,),
        in_specs=[_rows(tm, d), _rows(tm, ATT_WIDTH), _rows(tm, MLSTM_WIDTH), _rows(tm, d), _rows(tm, d),
                  _layer(wa.shape, layer), _layer(wb.shape, layer), _layer(wo.shape, layer)],
        out_specs=_rows(tm, d),
        compiler_params=_params(1),
        name="merge_out",
    )(x, ya, ym, ga, gm, wa, wb, wo)


def _w_in_offsets():
    o_v = ATT_WIDTH + 2 * ATT_KV_WIDTH
    o_km = o_v + 2 * MLSTM_WIDTH
    o_om = o_km + 2 * MLSTM_WIDTH
    o_g = o_om + 4 * MLSTM_HEADS
    return o_v, o_km, o_om, o_g


def _split_kernel(wt_ref, qkv_ref, qkm_ref, vo_ref, gate_ref, mg_ref):
    o_v, o_km, o_om, o_g = _w_in_offsets()
    qkv_ref[0] = wt_ref[0, :o_v, :].T.astype(BF16)
    qkm_ref[0] = wt_ref[0, o_v:o_km, :].T.astype(BF16)
    vo_ref[0] = wt_ref[0, o_km:o_om, :].T.astype(BF16)
    gate_ref[0] = wt_ref[0, o_om:o_g, :]
    mg_ref[0] = wt_ref[0, o_g:, :].T.astype(BF16)


def _split_w_in(w_in):
    depth, rows, cols = w_in.shape
    o_v, o_km, o_om, o_g = _w_in_offsets()
    widths = (o_v, o_km - o_v, o_om - o_km, cols - o_g)
    out_shape = [jax.ShapeDtypeStruct((depth, rows, w), BF16) for w in widths]
    out_specs = [pl.BlockSpec((1, CAST_ROWS, w), lambda l, i: (l, i, 0)) for w in widths]
    out_shape.insert(3, jax.ShapeDtypeStruct((depth, o_g - o_om, rows), F32))
    out_specs.insert(3, pl.BlockSpec((1, o_g - o_om, CAST_ROWS), lambda l, i: (l, 0, i)))
    return pl.pallas_call(
        _split_kernel,
        out_shape=out_shape,
        grid=(depth, rows // CAST_ROWS),
        in_specs=[pl.BlockSpec((1, cols, CAST_ROWS), lambda l, i: (l, 0, i))],
        out_specs=out_specs,
        compiler_params=_params(2),
        name="split_w_in",
    )(jnp.swapaxes(w_in, 1, 2))


def _proj_weights(wg_t, gate_bias, q_gain, k_gain):
    H = MLSTM_HEADS
    gate_row = jnp.concatenate([wg_t[0:H], wg_t[2 * H:3 * H], wg_t[H:2 * H], wg_t[3 * H:4 * H]], axis=0)
    b_i = jnp.concatenate([gate_bias[0:H], gate_bias[2 * H:3 * H]])
    b_f = jnp.concatenate([gate_bias[H:2 * H], gate_bias[3 * H:4 * H]])
    return {
        "gate_row": gate_row.astype(BF16),
        "bias_row": jnp.concatenate([b_i, b_f]).reshape(4 * H, 1).astype(F32),
        "q_gain": jnp.tile(q_gain, ATT_HEADS).reshape(1, ATT_WIDTH).astype(F32),
        "k_gain": jnp.tile(k_gain, ATT_KV_HEADS).reshape(1, ATT_KV_WIDTH).astype(F32),
    }


def kernel(x, positions, ffn1_norm, ffn1_w_gate, ffn1_w_up, ffn1_w_down, mix_norm, w_in, mlstm_gate_bias, attn_q_norm, attn_k_norm, attn_sink, mlstm_conv_w, mlstm_conv_b, mlstm_out_norm, w_branch_attn, w_branch_mlstm, w_out, ffn2_norm, ffn2_w_gate, ffn2_w_up, ffn2_w_down, block_out_norm):
    batch, seq, d = x.shape
    depth = w_in.shape[0]
    t = batch * seq
    xt = x.reshape(t, d)
    cos_t, sin_t = _rope_tables(positions)
    wg1, wu1, wg2, wu2 = _to_bf16(ffn1_w_gate, ffn1_w_up, ffn2_w_gate, ffn2_w_up)
    wd1, wd2 = _to_bf16(ffn1_w_down, ffn2_w_down)
    wa, wb = _to_bf16(w_branch_attn, w_branch_mlstm)
    (wo,) = _to_bf16(w_out)
    w_qkv, w_qkm, w_vo, w_gate, w_mg = _split_w_in(w_in)
    big = {"qkv": w_qkv, "qkm": w_qkm, "vo": w_vo, "merge": w_mg}
    for l in range(depth):
        xt = _ffn(xt, ffn1_norm[l], wg1, wu1, wd1, l)
        pw = _proj_weights(w_gate[l], mlstm_gate_bias[l], attn_q_norm[l], attn_k_norm[l])
        qa, k4, v4, qm, km, vm, og, grow, ga, gm = _proj(xt, mix_norm[l], big, l, pw, cos_t, sin_t)
        ya = _attention(qa, k4, v4, attn_sink[l].astype(F32), batch, seq)
        ym = _mlstm(qm, km, vm, og, grow, mlstm_conv_w[l].astype(F32),
                    mlstm_conv_b[l].reshape(1, -1).astype(F32),
                    mlstm_out_norm[l].reshape(1, -1).astype(F32), batch, seq)
        xt = _merge(xt, ya, ym, ga, gm, wa, wb, wo, l)
        xt = _ffn(xt, ffn2_norm[l], wg2, wu2, wd2, l, final_gain=block_out_norm[l])
    return xt.reshape(batch, seq, d)
```

```python
import functools

import numpy as np
import jax
import jax.numpy as jnp
from jax import lax
from jax.experimental import pallas as pl
from jax.experimental.pallas import tpu as pltpu

F32 = jnp.float32
BF16 = jnp.bfloat16

ATT_HEAD_DIM = 64
ATT_HEADS = 8
ATT_KV_HEADS = 2
ATT_WIDTH = ATT_HEADS * ATT_HEAD_DIM
ATT_KV_WIDTH = ATT_KV_HEADS * ATT_HEAD_DIM
WINDOW = 128
ATT_BLOCK = 128
ROPE_DIM = ATT_HEAD_DIM // 4
ROPE_THETA = 500000.0
MLSTM_HEADS = 4
MLSTM_HEAD_DIM = 128
MLSTM_WIDTH = MLSTM_HEADS * MLSTM_HEAD_DIM
MLSTM_CHUNK = 128
NORM_EPS = 1e-6
NEG_BIG = -1e30
LOG2E = 1.4426950408889634

V7X_LANES = 128
V7X_MXU_COLS = 256
V7X_VMEM_LIMIT_BYTES = 56 * 1024 * 1024

FFN_ROWS = 1024
PROJ_ROWS = 512
MERGE_ROWS = 1024
CAST_ROWS = 256
ROPE_ROWS = 2048


def _params(n_axes):
    return pltpu.CompilerParams(
        dimension_semantics=("parallel",) * n_axes,
        vmem_limit_bytes=V7X_VMEM_LIMIT_BYTES,
    )


def _resident(shape):
    nd = len(shape)
    return pl.BlockSpec(shape, lambda *_: (0,) * nd, pipeline_mode=pl.Buffered(1))


def _layer(shape, layer):
    return pl.BlockSpec((None,) + tuple(shape[1:]), lambda *_: (layer, 0, 0), pipeline_mode=pl.Buffered(1))


def _rows(tm, width):
    return pl.BlockSpec((tm, width), lambda i: (i, 0))


def _cast_kernel(*refs):
    n = len(refs) // 2
    for src, dst in zip(refs[:n], refs[n:]):
        dst[...] = src[...].astype(BF16)


def _to_bf16(*ws):
    depth, rows, cols = ws[0].shape
    assert all(w.shape == ws[0].shape for w in ws) and rows % CAST_ROWS == 0
    spec = pl.BlockSpec((1, CAST_ROWS, cols), lambda l, i: (l, i, 0))
    return pl.pallas_call(
        _cast_kernel,
        out_shape=[jax.ShapeDtypeStruct(w.shape, BF16) for w in ws],
        grid=(depth, rows // CAST_ROWS),
        in_specs=[spec] * len(ws),
        out_specs=[spec] * len(ws),
        compiler_params=_params(2),
        name="cast_bf16",
    )(*ws)


def _rms(x, gain):
    ms = jnp.mean(x * x, axis=-1, keepdims=True)
    return x * lax.rsqrt(ms + NORM_EPS) * gain


def _dot(a, b):
    return jnp.dot(a, b, preferred_element_type=F32)


def _dot_nt(a, b):
    return lax.dot_general(a, b, (((1,), (1,)), ((), ())), preferred_element_type=F32)


def _dot_tn(a, b):
    return lax.dot_general(a, b, (((0,), (0,)), ((), ())), preferred_element_type=F32)


def _sigmoid(x):
    return 0.5 * jnp.tanh(0.5 * x) + 0.5


def _log_sigmoid(x):
    return jnp.minimum(x, 0.0) - jnp.log1p(jnp.exp(-jnp.abs(x)))


def _rope_kernel(pos_ref, freq_ref, cos_ref, sin_ref):
    ang = pos_ref[...].astype(F32) * freq_ref[...]
    j = lax.broadcasted_iota(jnp.int32, ang.shape, 1) % ATT_HEAD_DIM
    half = ROPE_DIM // 2
    c = jnp.cos(ang)
    s = jnp.sin(ang)
    cos_ref[...] = jnp.where(j < ROPE_DIM, c, 1.0)
    sin_ref[...] = jnp.where(j < half, -s, jnp.where(j < ROPE_DIM, s, 0.0))


def _rope_tables(positions):
    t = positions.size
    half = ROPE_DIM // 2
    inv_freq = np.power(np.float32(ROPE_THETA),
                        -np.arange(half, dtype=np.float32) * np.float32(2.0 / ROPE_DIM)).astype(np.float32)
    lane = np.arange(V7X_LANES) % ATT_HEAD_DIM
    freq = np.where(lane < ROPE_DIM, inv_freq[lane % half], 0.0).astype(np.float32)[None, :]
    pos = positions.reshape(t, 1)
    return pl.pallas_call(
        _rope_kernel,
        out_shape=[jax.ShapeDtypeStruct((t, V7X_LANES), F32)] * 2,
        grid=(t // ROPE_ROWS,),
        in_specs=[_rows(ROPE_ROWS, 1), _resident((1, V7X_LANES))],
        out_specs=[_rows(ROPE_ROWS, V7X_LANES)] * 2,
        compiler_params=_params(1),
        name="rope_tables",
    )(pos, jnp.asarray(freq))


def _ffn_kernel(*refs, d_ff, final_norm):
    if final_norm:
        x_ref, g_ref, wg_ref, wu_ref, wd_ref, go_ref, o_ref, hn_ref, a_ref = refs
    else:
        x_ref, g_ref, wg_ref, wu_ref, wd_ref, o_ref, hn_ref, a_ref = refs
    hn_ref[...] = _rms(x_ref[...], g_ref[...]).astype(BF16)
    for c0 in range(0, d_ff, V7X_MXU_COLS):
        cols = slice(c0, c0 + V7X_MXU_COLS)
        hn = hn_ref[...]
        gate = _dot(hn, wg_ref[:, cols])
        up = _dot(hn, wu_ref[:, cols])
        a_ref[:, cols] = (gate * _sigmoid(gate) * up).astype(BF16)
    out = x_ref[...] + 0.5 * _dot(a_ref[...], wd_ref[...])
    if final_norm:
        out = _rms(out, go_ref[...])
    o_ref[...] = out


def _ffn(x, gain, wg, wu, wd, layer, final_gain=None):
    t, d = x.shape
    d_ff = wg.shape[2]
    assert d_ff % V7X_MXU_COLS == 0 and t % FFN_ROWS == 0
    final_norm = final_gain is not None
    args = [x, gain.reshape(1, d), wg, wu, wd]
    in_specs = [_rows(FFN_ROWS, d), _resident((1, d)), _layer(wg.shape, layer),
                _layer(wu.shape, layer), _layer(wd.shape, layer)]
    if final_norm:
        args.append(final_gain.reshape(1, d))
        in_specs.append(_resident((1, d)))
    return pl.pallas_call(
        functools.partial(_ffn_kernel, d_ff=d_ff, final_norm=final_norm),
        out_shape=jax.ShapeDtypeStruct((t, d), F32),
        grid=(t // FFN_ROWS,),
        in_specs=in_specs,
        out_specs=_rows(FFN_ROWS, d),
        scratch_shapes=[pltpu.VMEM((FFN_ROWS, d), BF16), pltpu.VMEM((FFN_ROWS, d_ff), BF16)],
        compiler_params=_params(1),
        name="ffn_final" if final_norm else "ffn",
    )(*args)


def _head_rms(x, ones_bd, gain):
    ss = _dot((x * x).astype(BF16), ones_bd)
    return x * lax.rsqrt(ss * (1.0 / ATT_HEAD_DIM) + NORM_EPS) * gain


def _rope(x, cos, sin):
    n = x.shape[1]
    half = ROPE_DIM // 2
    j = lax.broadcasted_iota(jnp.int32, x.shape, 1) % ATT_HEAD_DIM
    partner = jnp.where(j < half, pltpu.roll(x, n - half, 1), pltpu.roll(x, half, 1))
    return x * cos + partner * sin


def _lo_hi(x):
    lo = lax.broadcasted_iota(jnp.int32, x.shape, 1) < ATT_HEAD_DIM
    xr = pltpu.roll(x, ATT_HEAD_DIM, 1)
    zero = jnp.zeros_like(x)
    return jnp.concatenate([jnp.where(lo, x, zero), jnp.where(lo, zero, xr),
                            jnp.where(lo, xr, zero), jnp.where(lo, zero, x)], axis=1)


def _proj_kernel(x_ref, g_ref, wqkv_ref, wqkm_ref, wvo_ref, wgr_ref, bgr_ref,
                 wmg_ref, qg_ref, kg_ref, cos_ref, sin_ref, ones_ref,
                 qa_ref, k4_ref, v4_ref, qm_ref, km_ref, vm_ref, og_ref, gr_ref,
                 ga_ref, gmm_ref):
    hn = _rms(x_ref[...], g_ref[...]).astype(BF16)

    def proj(w):
        return _dot(hn, w)

    cos = cos_ref[...]
    sin = sin_ref[...]
    ones_bd = ones_ref[...]

    q = proj(wqkv_ref[:, :ATT_WIDTH])
    k = proj(wqkv_ref[:, ATT_WIDTH:ATT_WIDTH + ATT_KV_WIDTH])
    v = proj(wqkv_ref[:, ATT_WIDTH + ATT_KV_WIDTH:])
    qm_ref[...] = proj(wqkm_ref[:, :MLSTM_WIDTH]).astype(BF16)
    q = _head_rms(q, ones_bd, qg_ref[...])
    km_ref[...] = proj(wqkm_ref[:, MLSTM_WIDTH:]).astype(BF16)
    k = _head_rms(k, ones_bd[:ATT_KV_WIDTH, :ATT_KV_WIDTH], kg_ref[...])
    vm_ref[...] = proj(wvo_ref[:, :MLSTM_WIDTH]).astype(BF16)
    og_ref[...] = _sigmoid(proj(wvo_ref[:, MLSTM_WIDTH:])).astype(BF16)
    gr = _dot_nt(wgr_ref[...], hn) + bgr_ref[...]
    for c in range(gr_ref.shape[0]):
        gr_ref[c] = gr[:, c * MLSTM_CHUNK:(c + 1) * MLSTM_CHUNK]

    reps = ATT_WIDTH // V7X_LANES
    q = _rope(q, jnp.concatenate([cos] * reps, axis=1), jnp.concatenate([sin] * reps, axis=1))
    qa_ref[...] = (q * (ATT_HEAD_DIM ** -0.5 * LOG2E)).astype(BF16)
    k4_ref[...] = _lo_hi(_rope(k, cos, sin)).astype(BF16)
    v4_ref[...] = _lo_hi(v).astype(BF16)

    d = ga_ref.shape[1]
    piece = 2 * V7X_MXU_COLS
    for c0 in range(0, d, piece):
        ga_ref[:, c0:c0 + piece] = _sigmoid(proj(wmg_ref[:, c0:c0 + piece])).astype(BF16)
    for c0 in range(0, d, piece):
        gmm_ref[:, c0:c0 + piece] = _sigmoid(proj(wmg_ref[:, d + c0:d + c0 + piece])).astype(BF16)


def _proj(x, gain, big, layer, w, cos_t, sin_t):
    t, d = x.shape
    tm = PROJ_ROWS
    nck = tm // MLSTM_CHUNK
    heads = np.arange(ATT_WIDTH) // ATT_HEAD_DIM
    ones_bd = jnp.asarray((heads[:, None] == heads[None, :]).astype(np.float32), dtype=BF16)
    out_shape = [
        jax.ShapeDtypeStruct((t, ATT_WIDTH), BF16),
        jax.ShapeDtypeStruct((t, 4 * V7X_LANES), BF16),
        jax.ShapeDtypeStruct((t, 4 * V7X_LANES), BF16),
        jax.ShapeDtypeStruct((t, MLSTM_WIDTH), BF16),
        jax.ShapeDtypeStruct((t, MLSTM_WIDTH), BF16),
        jax.ShapeDtypeStruct((t, MLSTM_WIDTH), BF16),
        jax.ShapeDtypeStruct((t, MLSTM_WIDTH), BF16),
        jax.ShapeDtypeStruct((t // MLSTM_CHUNK, 16, MLSTM_CHUNK), F32),
        jax.ShapeDtypeStruct((t, d), BF16),
        jax.ShapeDtypeStruct((t, d), BF16),
    ]
    out_specs = [
        _rows(tm, ATT_WIDTH), _rows(tm, 4 * V7X_LANES), _rows(tm, 4 * V7X_LANES),
        _rows(tm, MLSTM_WIDTH), _rows(tm, MLSTM_WIDTH), _rows(tm, MLSTM_WIDTH), _rows(tm, MLSTM_WIDTH),
        pl.BlockSpec((nck, 16, MLSTM_CHUNK), lambda i: (i, 0, 0)),
        _rows(tm, d), _rows(tm, d),
    ]
    args = [x, gain.reshape(1, d), big["qkv"], big["qkm"], big["vo"], w["gate_row"],
            w["bias_row"], big["merge"], w["q_gain"], w["k_gain"], cos_t, sin_t, ones_bd]
    in_specs = [_rows(tm, d)] + [
        _layer(a.shape, layer) if a.ndim == 3 else _resident(a.shape) for a in args[1:10]] + [
        _rows(tm, V7X_LANES), _rows(tm, V7X_LANES), _resident(ones_bd.shape)]
    return pl.pallas_call(
        _proj_kernel,
        out_shape=out_shape,
        grid=(t // tm,),
        in_specs=in_specs,
        out_specs=out_specs,
        compiler_params=_params(1),
        name="mixer_proj",
    )(*args)


def _attn_kernel(sink_ref, q_ref, k4_ref, v4_ref, o_ref, *, seq):
    blk = ATT_BLOCK
    assert WINDOW == blk
    nb = seq // blk
    group = ATT_HEADS // ATT_KV_HEADS

    def body(n, carry):
        r0 = pl.multiple_of(n * blk, blk)
        left = pl.multiple_of(jnp.maximum(r0 - blk, 0), blk)
        right = pl.multiple_of(jnp.minimum(r0 + blk, seq - blk), blk)
        qb = q_ref[pl.ds(r0, blk), :]
        kb = jnp.concatenate([k4_ref[pl.ds(left, blk), :], k4_ref[pl.ds(r0, blk), :],
                              k4_ref[pl.ds(right, blk), :]], axis=0)
        vb = jnp.concatenate([v4_ref[pl.ds(left, blk), :], v4_ref[pl.ds(r0, blk), :],
                              v4_ref[pl.ds(right, blk), :]], axis=0)
        ql = lax.broadcasted_iota(jnp.int32, (blk, blk), 0)
        kl = lax.broadcasted_iota(jnp.int32, (blk, blk), 1)
        left_ok = kl >= ql + jnp.where(n > 0, 0, blk)
        right_ok = kl + jnp.where(n < nb - 1, 0, blk) <= ql
        lo = lax.broadcasted_iota(jnp.int32, (blk, V7X_LANES), 1) < ATT_HEAD_DIM
        lo_band = lax.broadcasted_iota(jnp.int32, (3 * blk, V7X_LANES), 1) < ATT_HEAD_DIM
        ones_lo = jnp.where(lo_band, 1.0, 0.0).astype(BF16)
        ones_hi = jnp.where(lo_band, 0.0, 1.0).astype(BF16)
        outs = []
        for g in range(ATT_KV_HEADS):
            kv_lo = slice(2 * g * V7X_LANES, (2 * g + 1) * V7X_LANES)
            kv_hi = slice((2 * g + 1) * V7X_LANES, (2 * g + 2) * V7X_LANES)
            v_aug = jnp.concatenate([jnp.concatenate([vb[:, kv_lo], ones_lo], axis=1),
                                     jnp.concatenate([vb[:, kv_hi], ones_hi], axis=1)], axis=0)
            k_both = jnp.concatenate([kb[:, kv_lo], kb[:, kv_hi]], axis=0)
            for p in range(group // 2):
                pair = g * (group // 2) + p
                qp = qb[:, pair * V7X_LANES:(pair + 1) * V7X_LANES]
                s_both = _dot_nt(qp, k_both)
                probs, shifts = [], []
                for half in range(2):
                    sink = sink_ref[2 * pair + half] * LOG2E
                    s = s_both[:, half * 3 * blk:(half + 1) * 3 * blk]
                    s_l = jnp.where(left_ok, s[:, :blk], NEG_BIG)
                    s_m = s[:, blk:2 * blk]
                    s_r = jnp.where(right_ok, s[:, 2 * blk:], NEG_BIG)
                    m = jnp.max(jnp.maximum(jnp.maximum(s_l, s_r), s_m), axis=-1, keepdims=True)
                    m = jnp.maximum(m, sink)
                    probs += [jnp.exp2(s_l - m), jnp.exp2(s_m - m), jnp.exp2(s_r - m)]
                    shifts.append(sink - m)
                acc = _dot(jnp.concatenate(probs, axis=1).astype(BF16), v_aug)
                den = acc[:, V7X_LANES:] + jnp.exp2(jnp.where(lo, shifts[0], shifts[1]))
                outs.append(acc[:, :V7X_LANES] / den)
        o_ref[pl.ds(r0, blk), :] = jnp.concatenate(outs, axis=1).astype(BF16)
        return carry

    lax.fori_loop(0, nb, body, 0, unroll=16)


def _attention(q, k4, v4, sink, batch, seq):
    t = q.shape[0]
    by_batch = lambda w: pl.BlockSpec((seq, w), lambda b: (b, 0))
    return pl.pallas_call(
        functools.partial(_attn_kernel, seq=seq),
        out_shape=jax.ShapeDtypeStruct((t, ATT_WIDTH), BF16),
        grid=(batch,),
        in_specs=[pl.BlockSpec(memory_space=pltpu.SMEM), by_batch(ATT_WIDTH),
                  by_batch(4 * V7X_LANES), by_batch(4 * V7X_LANES)],
        out_specs=by_batch(ATT_WIDTH),
        compiler_params=_params(1),
        name="window_attn",
    )(sink, q, k4, v4)


def _scan(x, axis, reverse, op, identity):
    n = x.shape[axis]
    idx = lax.broadcasted_iota(jnp.int32, x.shape, axis)
    d = 1
    while d < n:
        if reverse:
            x = op(x, jnp.where(idx < n - d, pltpu.roll(x, n - d, axis), identity))
        else:
            x = op(x, jnp.where(idx >= d, pltpu.roll(x, d, axis), identity))
        d *= 2
    return x


def _mlstm_kernel(qraw_ref, kraw_ref, v_ref, og_ref, gr_ref, cw_ref, cb_ref, ng_ref,
                  y_ref,
                  ks_ref, qt_ref, kt_ref, vt_ref, rcol_ref, b_ref, dm_ref, w_ref, r_ref, am_ref, bt_ref,
                  cst_ref, mst_ref, cin_ref, minf_ref, minb_ref, hs_ref, *, seq):
    L = MLSTM_CHUNK
    H = MLSTM_HEADS
    dh = MLSTM_HEAD_DIM
    nc = seq // L
    width = MLSTM_WIDTH
    halo = 16
    row8 = lax.broadcasted_iota(jnp.int32, (8, L), 0)

    def head(h):
        return slice(h * dh, (h + 1) * dh)

    assert nc * 8 == V7X_LANES and L == V7X_LANES
    shape = (nc * 8, L)
    gi = gr_ref[:, 0:8, :].reshape(shape)
    lf = _log_sigmoid(gr_ref[:, 8:16, :].reshape(shape))
    fwd = (lax.broadcasted_iota(jnp.int32, shape, 0) & 7) < H
    lane = lax.broadcasted_iota(jnp.int32, shape, 1)
    b = jnp.where(fwd, _scan(lf, 1, False, jnp.add, 0.0), _scan(lf, 1, True, jnp.add, 0.0))
    b_tot = jnp.sum(jnp.where(lane == jnp.where(fwd, L - 1, 0), b, 0.0), axis=1, keepdims=True)
    a = b_tot - b + gi
    a_max = jnp.max(a, axis=1, keepdims=True)
    r = gi - b
    d_max = b + jnp.where(fwd, _scan(r, 1, False, jnp.maximum, NEG_BIG),
                          _scan(r, 1, True, jnp.maximum, NEG_BIG))
    r_ref[...] = (r * LOG2E).reshape(nc, 8, L)
    b_ref[...] = (b * LOG2E).reshape(nc, 8, L)
    dm_ref[...] = (d_max * LOG2E).reshape(nc, 8, L)
    w_ref[...] = jnp.exp(a - a_max).reshape(nc, 8, L)
    am_ref[...] = jnp.broadcast_to(a_max, shape).reshape(nc, 8, L)
    bt_ref[...] = jnp.broadcast_to(b_tot, shape).reshape(nc, 8, L)

    def prep(c, carry):
        r0 = pl.multiple_of(c * L, L)
        rows = pl.ds(r0, L)
        rid = lax.broadcasted_iota(jnp.int32, (L, width), 0)
        has_prev = jnp.where(c > 0, 1.0, 0.0)
        has_next = jnp.where(c < nc - 1, 1.0, 0.0)
        prev_at = pl.multiple_of(jnp.maximum(r0 - halo, 0), halo)
        next_at = pl.multiple_of(jnp.minimum(r0 + L, seq - halo), halo)
        acts = []
        for idx, raw_ref in enumerate((qraw_ref, kraw_ref)):
            cols = slice(idx * width, (idx + 1) * width)
            cur = raw_ref[rows, :].astype(F32)
            prev_row = raw_ref[pl.ds(prev_at, halo), :].astype(F32)[halo - 1:halo, :] * has_prev
            next_row = raw_ref[pl.ds(next_at, halo), :].astype(F32)[0:1, :] * has_next
            before = jnp.where(rid == 0, prev_row, pltpu.roll(cur, 1, 0))
            after = jnp.where(rid == L - 1, next_row, pltpu.roll(cur, L - 1, 0))
            u = (before * cw_ref[0:1, cols] + cur * cw_ref[1:2, cols] + after * cw_ref[2:3, cols]
                 + cb_ref[:, cols])
            u = u / (1.0 + jnp.exp2(u * (-LOG2E)))
            if idx == 1:
                u = u * (dh ** -0.5)
            acts.append(u)
        q_act, k_act = acts
        ks_ref[rows, :] = k_act.astype(BF16)
        ones_rows = jnp.ones((halo, L), BF16)
        for h in range(H):
            qt_ref[c, h] = q_act[:, head(h)].T.astype(BF16)
            kt_ref[c, h] = k_act[:, head(h)].T.astype(BF16)
            vt_ref[c, h, 0:dh, :] = v_ref[rows, head(h)].astype(F32).T.astype(BF16)
            vt_ref[c, h, dh:dh + halo, :] = ones_rows
        rcol_ref[rows, :] = jnp.concatenate([r_ref[c], jnp.zeros((L - 8, L), F32)], axis=0).T
        return carry

    lax.fori_loop(0, nc, prep, 0, unroll=4)

    cst_ref[...] = jnp.zeros_like(cst_ref)
    mst_ref[...] = jnp.zeros_like(mst_ref)

    def scan(i, carry):
        c_f = i
        c_b = nc - 1 - i
        fwd = row8 < H
        bt = jnp.where(fwd, bt_ref[c_f], bt_ref[c_b])
        am = jnp.where(fwd, am_ref[c_f], am_ref[c_b])
        m_old = mst_ref[...]
        m_new = jnp.maximum(bt + m_old, am)
        s_prev = jnp.exp(bt + m_old - m_new)
        s_loc = jnp.exp(am - m_new)
        minf_ref[c_f] = m_old
        minb_ref[c_b] = m_old
        mst_ref[...] = m_new
        for d, c in enumerate((c_f, c_b)):
            wr = w_ref[c]
            for h in range(H):
                j = d * H + h
                kw_t = (kt_ref[c, h].astype(F32) * wr[j:j + 1, :]).astype(BF16)
                c_loc = _dot_nt(vt_ref[c, h], kw_t)
                c_old = cst_ref[j]
                cin_ref[c, j] = c_old.astype(BF16)
                cst_ref[j] = s_prev[j:j + 1, :] * c_old + s_loc[j:j + 1, :] * c_loc
        return carry

    lax.fori_loop(0, nc, scan, 0, unroll=8)

    def emit(c):
        rows = pl.ds(pl.multiple_of(c * L, L), L)
        m_in = jnp.where(row8 < H, minf_ref[c], minb_ref[c]) * LOG2E
        bc = b_ref[c]
        m_t = jnp.maximum(bc + m_in, dm_ref[c])
        b_rel = bc - m_t
        scale_in = jnp.exp2(b_rel + m_in)
        floor = jnp.exp2(-m_t)
        rcol = rcol_ref[rows, :]
        s_id = lax.broadcasted_iota(jnp.int32, (L, L), 0)
        t_id = lax.broadcasted_iota(jnp.int32, (L, L), 1)
        masks = (s_id <= t_id, s_id >= t_id)
        for h in range(H):
            q_t = qt_ref[c, h]
            s_t = _dot(ks_ref[rows, head(h)], q_t)
            weights, sums, inters = [], [], []
            for d in range(2):
                j = d * H + h
                logw = jnp.broadcast_to(rcol[:, j:j + 1], (L, L)) + b_rel[j:j + 1, :]
                sc = s_t * jnp.exp2(jnp.where(masks[d], logw, NEG_BIG))
                sums.append(jnp.sum(sc, axis=0, keepdims=True))
                weights.append(sc.astype(BF16))
                inters.append(_dot(cin_ref[c, j], q_t))
            num = _dot(vt_ref[c, h, 0:dh, :], jnp.concatenate(weights, axis=1))
            hsum = None
            for d in range(2):
                j = d * H + h
                den = sums[d] + scale_in[j:j + 1, :] * inters[d][dh:dh + 1, :]
                inv = 1.0 / jnp.maximum(jnp.abs(den), floor[j:j + 1, :])
                hd = num[:, d * L:(d + 1) * L] * inv + inters[d][0:dh, :] * (scale_in[j:j + 1, :] * inv)
                hsum = hd if hsum is None else hsum + hd
            hs_ref[c, h] = hsum

    def finish(c):
        rows = pl.ds(pl.multiple_of(c * L, L), L)
        for h in range(H):
            hsum = hs_ref[c, h]
            xc = hsum - jnp.mean(hsum, axis=0, keepdims=True)
            var = jnp.mean(xc * xc, axis=0, keepdims=True)
            y = (xc * lax.rsqrt(var + NORM_EPS)).T * ng_ref[:, head(h)] * og_ref[rows, head(h)].astype(F32)
            y_ref[rows, head(h)] = y.astype(BF16)

    def emit_and_finish(c, carry):
        finish(c - 1)
        emit(c)
        return carry

    emit(0)
    lax.fori_loop(1, nc, emit_and_finish, 0, unroll=5)
    finish(nc - 1)


def _mlstm(qraw, kraw, v, og, grow, conv_w, conv_b, norm_g, batch, seq):
    t = qraw.shape[0]
    nc = seq // MLSTM_CHUNK
    nd = 2 * MLSTM_HEADS
    by_batch = lambda w: pl.BlockSpec((seq, w), lambda b: (b, 0))
    dh = MLSTM_HEAD_DIM
    aug = dh + 16
    tile = (nc, MLSTM_HEADS, dh, MLSTM_CHUNK)
    rowvec = pltpu.VMEM((nc, 8, MLSTM_CHUNK), F32)
    scratch = [
        pltpu.VMEM((seq, MLSTM_WIDTH), BF16),
        pltpu.VMEM(tile, BF16),
        pltpu.VMEM(tile, BF16),
        pltpu.VMEM((nc, MLSTM_HEADS, aug, MLSTM_CHUNK), BF16),
        pltpu.VMEM((seq, V7X_LANES), F32),
        rowvec,
        rowvec,
        rowvec,
        rowvec,
        rowvec,
        rowvec,
        pltpu.VMEM((nd, aug, dh), F32),
        pltpu.VMEM((8, MLSTM_CHUNK), F32),
        pltpu.VMEM((nc, nd, aug, dh), BF16),
        rowvec,
        rowvec,
        pltpu.VMEM(tile, F32),
    ]
    return pl.pallas_call(
        functools.partial(_mlstm_kernel, seq=seq),
        out_shape=jax.ShapeDtypeStruct((t, MLSTM_WIDTH), BF16),
        grid=(batch,),
        in_specs=[by_batch(MLSTM_WIDTH)] * 4 + [
            pl.BlockSpec((nc, 16, MLSTM_CHUNK), lambda b: (b, 0, 0)),
            _resident(conv_w.shape), _resident(conv_b.shape), _resident(norm_g.shape)],
        out_specs=by_batch(MLSTM_WIDTH),
        scratch_shapes=scratch,
        compiler_params=_params(1),
        name="bidir_mlstm",
    )(qraw, kraw, v, og, grow, conv_w, conv_b, norm_g)


def _merge_kernel(x_ref, ya_ref, ym_ref, ga_ref, gm_ref, wa_ref, wb_ref, wo_ref, o_ref):
    merged = (ga_ref[...].astype(F32) * _dot(ya_ref[...], wa_ref[...])
              + gm_ref[...].astype(F32) * _dot(ym_ref[...], wb_ref[...]))
    o_ref[...] = x_ref[...] + _dot(merged.astype(BF16), wo_ref[...])


def _merge(x, ya, ym, ga, gm, wa, wb, wo, layer):
    t, d = x.shape
    tm = MERGE_ROWS
    return pl.pallas_call(
        _merge_kernel,
        out_shape=jax.ShapeDtypeStruct((t, d), F32),
        grid=(t // tm,),
        in_specs=[_rows(tm, d), _rows(tm, ATT_WIDTH), _rows(tm, MLSTM_WIDTH), _rows(tm, d), _rows(tm, d),
                  _layer(wa.shape, layer), _layer(wb.shape, layer), _layer(wo.shape, layer)],
        out_specs=_rows(tm, d),
        compiler_params=_params(1),
        name="merge_out",
    )(x, ya, ym, ga, gm, wa, wb, wo)


def _w_in_offsets():
    o_v = ATT_WIDTH + 2 * ATT_KV_WIDTH
    o_km = o_v + 2 * MLSTM_WIDTH
    o_om = o_km + 2 * MLSTM_WIDTH
    o_g = o_om + 4 * MLSTM_HEADS
    return o_v, o_km, o_om, o_g


def _split_kernel(wt_ref, qkv_ref, qkm_ref, vo_ref, gate_ref, mg_ref):
    o_v, o_km, o_om, o_g = _w_in_offsets()
    qkv_ref[0] = wt_ref[0, :o_v, :].T.astype(BF16)
    qkm_ref[0] = wt_ref[0, o_v:o_km, :].T.astype(BF16)
    vo_ref[0] = wt_ref[0, o_km:o_om, :].T.astype(BF16)
    gate_ref[0] = wt_ref[0, o_om:o_g, :]
    mg_ref[0] = wt_ref[0, o_g:, :].T.astype(BF16)


def _split_w_in(w_in):
    depth, rows, cols = w_in.shape
    o_v, o_km, o_om, o_g = _w_in_offsets()
    widths = (o_v, o_km - o_v, o_om - o_km, cols - o_g)
    out_shape = [jax.ShapeDtypeStruct((depth, rows, w), BF16) for w in widths]
    out_specs = [pl.BlockSpec((1, CAST_ROWS, w), lambda l, i: (l, i, 0)) for w in widths]
    out_shape.insert(3, jax.ShapeDtypeStruct((depth, o_g - o_om, rows), F32))
    out_specs.insert(3, pl.BlockSpec((1, o_g - o_om, CAST_ROWS), lambda l, i: (l, 0, i)))
    return pl.pallas_call(
        _split_kernel,
        out_shape=out_shape,
        grid=(depth, rows // CAST_ROWS),
        in_specs=[pl.BlockSpec((1, cols, CAST_ROWS), lambda l, i: (l, 0, i))],
        out_specs=out_specs,
        compiler_params=_params(2),
        name="split_w_in",
    )(jnp.swapaxes(w_in, 1, 2))


def _proj_weights(wg_t, gate_bias, q_gain, k_gain):
    H = MLSTM_HEADS
    gate_row = jnp.concatenate([wg_t[0:H], wg_t[2 * H:3 * H], wg_t[H:2 * H], wg_t[3 * H:4 * H]], axis=0)
    b_i = jnp.concatenate([gate_bias[0:H], gate_bias[2 * H:3 * H]])
    b_f = jnp.concatenate([gate_bias[H:2 * H], gate_bias[3 * H:4 * H]])
    return {
        "gate_row": gate_row.astype(BF16),
        "bias_row": jnp.concatenate([b_i, b_f]).reshape(4 * H, 1).astype(F32),
        "q_gain": jnp.tile(q_gain, ATT_HEADS).reshape(1, ATT_WIDTH).astype(F32),
        "k_gain": jnp.tile(k_gain, ATT_KV_HEADS).reshape(1, ATT_KV_WIDTH).astype(F32),
    }


def kernel(x, positions, ffn1_norm, ffn1_w_gate, ffn1_w_up, ffn1_w_down, mix_norm, w_in, mlstm_gate_bias, attn_q_norm, attn_k_norm, attn_sink, mlstm_conv_w, mlstm_conv_b, mlstm_out_norm, w_branch_attn, w_branch_mlstm, w_out, ffn2_norm, ffn2_w_gate, ffn2_w_up, ffn2_w_down, block_out_norm):
    batch, seq, d = x.shape
    depth = w_in.shape[0]
    t = batch * seq
    xt = x.reshape(t, d)
    cos_t, sin_t = _rope_tables(positions)
    wg1, wu1, wg2, wu2 = _to_bf16(ffn1_w_gate, ffn1_w_up, ffn2_w_gate, ffn2_w_up)
    wd1, wd2 = _to_bf16(ffn1_w_down, ffn2_w_down)
    wa, wb = _to_bf16(w_branch_attn, w_branch_mlstm)
    (wo,) = _to_bf16(w_out)
    w_qkv, w_qkm, w_vo, w_gate, w_mg = _split_w_in(w_in)
    big = {"qkv": w_qkv, "qkm": w_qkm, "vo": w_vo, "merge": w_mg}
    for l in range(depth):
        xt = _ffn(xt, ffn1_norm[l], wg1, wu1, wd1, l)
        pw = _proj_weights(w_gate[l], mlstm_gate_bias[l], attn_q_norm[l], attn_k_norm[l])
        qa, k4, v4, qm, km, vm, og, grow, ga, gm = _proj(xt, mix_norm[l], big, l, pw, cos_t, sin_t)
        ya = _attention(qa, k4, v4, attn_sink[l].astype(F32), batch, seq)
        ym = _mlstm(qm, km, vm, og, grow, mlstm_conv_w[l].astype(F32),
                    mlstm_conv_b[l].reshape(1, -1).astype(F32),
                    mlstm_out_norm[l].reshape(1, -1).astype(F32), batch, seq)
        xt = _merge(xt, ya, ym, ga, gm, wa, wb, wo, l)
        xt = _ffn(xt, ffn2_norm[l], wg2, wu2, wd2, l, final_gain=block_out_norm[l])
    return xt.reshape(batch, seq, d)
```

```python
import functools

import numpy as np
import jax
import jax.numpy as jnp
from jax import lax
from jax.experimental import pallas as pl
from jax.experimental.pallas import tpu as pltpu

F32 = jnp.float32
BF16 = jnp.bfloat16

ATT_HEAD_DIM = 64
ATT_HEADS = 8
ATT_KV_HEADS = 2
ATT_WIDTH = ATT_HEADS * ATT_HEAD_DIM
ATT_KV_WIDTH = ATT_KV_HEADS * ATT_HEAD_DIM
WINDOW = 128
ATT_BLOCK = 128
ROPE_DIM = ATT_HEAD_DIM // 4
ROPE_THETA = 500000.0
MLSTM_HEADS = 4
MLSTM_HEAD_DIM = 128
MLSTM_WIDTH = MLSTM_HEADS * MLSTM_HEAD_DIM
MLSTM_CHUNK = 128
NORM_EPS = 1e-6
NEG_BIG = -1e30
LOG2E = 1.4426950408889634

V7X_LANES = 128
V7X_MXU_COLS = 256
V7X_VMEM_LIMIT_BYTES = 56 * 1024 * 1024

FFN_ROWS = 1024
PROJ_ROWS = 512
MERGE_ROWS = 1024
CAST_ROWS = 256
ROPE_ROWS = 2048


def _params(n_axes):
    return pltpu.CompilerParams(
        dimension_semantics=("parallel",) * n_axes,
        vmem_limit_bytes=V7X_VMEM_LIMIT_BYTES,
    )


def _resident(shape):
    nd = len(shape)
    return pl.BlockSpec(shape, lambda *_: (0,) * nd, pipeline_mode=pl.Buffered(1))


def _layer(shape, layer):
    return pl.BlockSpec((None,) + tuple(shape[1:]), lambda *_: (layer, 0, 0), pipeline_mode=pl.Buffered(1))


def _rows(tm, width):
    return pl.BlockSpec((tm, width), lambda i: (i, 0))


def _cast_kernel(*refs):
    n = len(refs) // 2
    for src, dst in zip(refs[:n], refs[n:]):
        dst[...] = src[...].astype(BF16)


def _to_bf16(*ws):
    depth, rows, cols = ws[0].shape
    assert all(w.shape == ws[0].shape for w in ws) and rows % CAST_ROWS == 0
    spec = pl.BlockSpec((1, CAST_ROWS, cols), lambda l, i: (l, i, 0))
    return pl.pallas_call(
        _cast_kernel,
        out_shape=[jax.ShapeDtypeStruct(w.shape, BF16) for w in ws],
        grid=(depth, rows // CAST_ROWS),
        in_specs=[spec] * len(ws),
        out_specs=[spec] * len(ws),
        compiler_params=_params(2),
        name="cast_bf16",
    )(*ws)


def _rms(x, gain):
    ms = jnp.mean(x * x, axis=-1, keepdims=True)
    return x * lax.rsqrt(ms + NORM_EPS) * gain


def _dot(a, b):
    return jnp.dot(a, b, preferred_element_type=F32)


def _dot_nt(a, b):
    return lax.dot_general(a, b, (((1,), (1,)), ((), ())), preferred_element_type=F32)


def _sigmoid(x):
    return 0.5 * jnp.tanh(0.5 * x) + 0.5


def _log_sigmoid(x):
    return jnp.minimum(x, 0.0) - jnp.log1p(jnp.exp(-jnp.abs(x)))


def _rope_kernel(pos_ref, freq_ref, cos_ref, sin_ref):
    ang = pos_ref[...].astype(F32) * freq_ref[...]
    j = lax.broadcasted_iota(jnp.int32, ang.shape, 1) % ATT_HEAD_DIM
    half = ROPE_DIM // 2
    c = jnp.cos(ang)
    s = jnp.sin(ang)
    cos_ref[...] = jnp.where(j < ROPE_DIM, c, 1.0)
    sin_ref[...] = jnp.where(j < half, -s, jnp.where(j < ROPE_DIM, s, 0.0))


def _rope_tables(positions):
    t = positions.size
    half = ROPE_DIM // 2
    inv_freq = np.power(np.float32(ROPE_THETA),
                        -np.arange(half, dtype=np.float32) * np.float32(2.0 / ROPE_DIM)).astype(np.float32)
    lane = np.arange(V7X_LANES) % ATT_HEAD_DIM
    freq = np.where(lane < ROPE_DIM, inv_freq[lane % half], 0.0).astype(np.float32)[None, :]
    pos = positions.reshape(t, 1)
    return pl.pallas_call(
        _rope_kernel,
        out_shape=[jax.ShapeDtypeStruct((t, V7X_LANES), F32)] * 2,
        grid=(t // ROPE_ROWS,),
        in_specs=[_rows(ROPE_ROWS, 1), _resident((1, V7X_LANES))],
        out_specs=[_rows(ROPE_ROWS, V7X_LANES)] * 2,
        compiler_params=_params(1),
        name="rope_tables",
    )(pos, jnp.asarray(freq))


def _ffn_kernel(*refs, d_ff, final_norm):
    if final_norm:
        x_ref, g_ref, wg_ref, wu_ref, wd_ref, go_ref, o_ref, hn_ref, a_ref = refs
    else:
        x_ref, g_ref, wg_ref, wu_ref, wd_ref, o_ref, hn_ref, a_ref = refs
    hn_ref[...] = _rms(x_ref[...], g_ref[...]).astype(BF16)
    for c0 in range(0, d_ff, V7X_MXU_COLS):
        cols = slice(c0, c0 + V7X_MXU_COLS)
        hn = hn_ref[...]
        gate = _dot(hn, wg_ref[:, cols])
        up = _dot(hn, wu_ref[:, cols])
        a_ref[:, cols] = (gate * _sigmoid(gate) * up).astype(BF16)
    out = x_ref[...] + 0.5 * _dot(a_ref[...], wd_ref[...])
    if final_norm:
        out = _rms(out, go_ref[...])
    o_ref[...] = out


def _ffn(x, gain, wg, wu, wd, layer, final_gain=None):
    t, d = x.shape
    d_ff = wg.shape[2]
    assert d_ff % V7X_MXU_COLS == 0 and t % FFN_ROWS == 0
    final_norm = final_gain is not None
    args = [x, gain.reshape(1, d), wg, wu, wd]
    in_specs = [_rows(FFN_ROWS, d), _resident((1, d)), _layer(wg.shape, layer),
                _layer(wu.shape, layer), _layer(wd.shape, layer)]
    if final_norm:
        args.append(final_gain.reshape(1, d))
        in_specs.append(_resident((1, d)))
    return pl.pallas_call(
        functools.partial(_ffn_kernel, d_ff=d_ff, final_norm=final_norm),
        out_shape=jax.ShapeDtypeStruct((t, d), F32),
        grid=(t // FFN_ROWS,),
        in_specs=in_specs,
        out_specs=_rows(FFN_ROWS, d),
        scratch_shapes=[pltpu.VMEM((FFN_ROWS, d), BF16), pltpu.VMEM((FFN_ROWS, d_ff), BF16)],
        compiler_params=_params(1),
        name="ffn_final" if final_norm else "ffn",
    )(*args)


def _head_rms(x, ones_bd, gain):
    ss = _dot((x * x).astype(BF16), ones_bd)
    return x * lax.rsqrt(ss * (1.0 / ATT_HEAD_DIM) + NORM_EPS) * gain


def _rope(x, cos, sin):
    n = x.shape[1]
    half = ROPE_DIM // 2
    j = lax.broadcasted_iota(jnp.int32, x.shape, 1) % ATT_HEAD_DIM
    partner = jnp.where(j < half, pltpu.roll(x, n - half, 1), pltpu.roll(x, half, 1))
    return x * cos + partner * sin


def _lo_hi(x):
    lo = lax.broadcasted_iota(jnp.int32, x.shape, 1) < ATT_HEAD_DIM
    xr = pltpu.roll(x, ATT_HEAD_DIM, 1)
    zero = jnp.zeros_like(x)
    return jnp.concatenate([jnp.where(lo, x, zero), jnp.where(lo, zero, xr),
                            jnp.where(lo, xr, zero), jnp.where(lo, zero, x)], axis=1)


def _proj_kernel(x_ref, g_ref, wqkv_ref, wqkm_ref, wvo_ref, wgr_ref, bgr_ref,
                 wmg_ref, qg_ref, kg_ref, cos_ref, sin_ref, ones_ref,
                 qa_ref, k4_ref, v4_ref, qm_ref, km_ref, vm_ref, og_ref, gr_ref,
                 ga_ref, gmm_ref):
    hn = _rms(x_ref[...], g_ref[...]).astype(BF16)

    def proj(w):
        return _dot(hn, w)

    cos = cos_ref[...]
    sin = sin_ref[...]
    ones_bd = ones_ref[...]

    q = proj(wqkv_ref[:, :ATT_WIDTH])
    k = proj(wqkv_ref[:, ATT_WIDTH:ATT_WIDTH + ATT_KV_WIDTH])
    v = proj(wqkv_ref[:, ATT_WIDTH + ATT_KV_WIDTH:])
    qm_ref[...] = proj(wqkm_ref[:, :MLSTM_WIDTH]).astype(BF16)
    q = _head_rms(q, ones_bd, qg_ref[...])
    km_ref[...] = proj(wqkm_ref[:, MLSTM_WIDTH:]).astype(BF16)
    k = _head_rms(k, ones_bd[:ATT_KV_WIDTH, :ATT_KV_WIDTH], kg_ref[...])
    vm_ref[...] = proj(wvo_ref[:, :MLSTM_WIDTH]).astype(BF16)
    og_ref[...] = _sigmoid(proj(wvo_ref[:, MLSTM_WIDTH:])).astype(BF16)
    gr = _dot_nt(wgr_ref[...], hn) + bgr_ref[...]
    for c in range(gr_ref.shape[0]):
        gr_ref[c] = gr[:, c * MLSTM_CHUNK:(c + 1) * MLSTM_CHUNK]

    reps = ATT_WIDTH // V7X_LANES
    q = _rope(q, jnp.concatenate([cos] * reps, axis=1), jnp.concatenate([sin] * reps, axis=1))
    qa_ref[...] = (q * (ATT_HEAD_DIM ** -0.5 * LOG2E)).astype(BF16)
    k4_ref[...] = _lo_hi(_rope(k, cos, sin)).astype(BF16)
    v4_ref[...] = _lo_hi(v).astype(BF16)

    d = ga_ref.shape[1]
    piece = 2 * V7X_MXU_COLS
    for c0 in range(0, d, piece):
        ga_ref[:, c0:c0 + piece] = _sigmoid(proj(wmg_ref[:, c0:c0 + piece])).astype(BF16)
    for c0 in range(0, d, piece):
        gmm_ref[:, c0:c0 + piece] = _sigmoid(proj(wmg_ref[:, d + c0:d + c0 + piece])).astype(BF16)


def _proj(x, gain, big, layer, w, cos_t, sin_t):
    t, d = x.shape
    tm = PROJ_ROWS
    nck = tm // MLSTM_CHUNK
    heads = np.arange(ATT_WIDTH) // ATT_HEAD_DIM
    ones_bd = jnp.asarray((heads[:, None] == heads[None, :]).astype(np.float32), dtype=BF16)
    out_shape = [
        jax.ShapeDtypeStruct((t, ATT_WIDTH), BF16),
        jax.ShapeDtypeStruct((t, 4 * V7X_LANES), BF16),
        jax.ShapeDtypeStruct((t, 4 * V7X_LANES), BF16),
        jax.ShapeDtypeStruct((t, MLSTM_WIDTH), BF16),
        jax.ShapeDtypeStruct((t, MLSTM_WIDTH), BF16),
        jax.ShapeDtypeStruct((t, MLSTM_WIDTH), BF16),
        jax.ShapeDtypeStruct((t, MLSTM_WIDTH), BF16),
        jax.ShapeDtypeStruct((t // MLSTM_CHUNK, 16, MLSTM_CHUNK), F32),
        jax.ShapeDtypeStruct((t, d), BF16),
        jax.ShapeDtypeStruct((t, d), BF16),
    ]
    out_specs = [
        _rows(tm, ATT_WIDTH), _rows(tm, 4 * V7X_LANES), _rows(tm, 4 * V7X_LANES),
        _rows(tm, MLSTM_WIDTH), _rows(tm, MLSTM_WIDTH), _rows(tm, MLSTM_WIDTH), _rows(tm, MLSTM_WIDTH),
        pl.BlockSpec((nck, 16, MLSTM_CHUNK), lambda i: (i, 0, 0)),
        _rows(tm, d), _rows(tm, d),
    ]
    args = [x, gain.reshape(1, d), big["qkv"], big["qkm"], big["vo"], w["gate_row"],
            w["bias_row"], big["merge"], w["q_gain"], w["k_gain"], cos_t, sin_t, ones_bd]
    in_specs = [_rows(tm, d)] + [
        _layer(a.shape, layer) if a.ndim == 3 else _resident(a.shape) for a in args[1:10]] + [
        _rows(tm, V7X_LANES), _rows(tm, V7X_LANES), _resident(ones_bd.shape)]
    return pl.pallas_call(
        _proj_kernel,
        out_shape=out_shape,
        grid=(t // tm,),
        in_specs=in_specs,
        out_specs=out_specs,
        compiler_params=_params(1),
        name="mixer_proj",
    )(*args)


def _attn_kernel(sink_ref, q_ref, k4_ref, v4_ref, o_ref, *, seq):
    blk = ATT_BLOCK
    assert WINDOW == blk
    nb = seq // blk
    group = ATT_HEADS // ATT_KV_HEADS

    def body(n, carry):
        r0 = pl.multiple_of(n * blk, blk)
        left = pl.multiple_of(jnp.maximum(r0 - blk, 0), blk)
        right = pl.multiple_of(jnp.minimum(r0 + blk, seq - blk), blk)
        qb = q_ref[pl.ds(r0, blk), :]
        kb = jnp.concatenate([k4_ref[pl.ds(left, blk), :], k4_ref[pl.ds(r0, blk), :],
                              k4_ref[pl.ds(right, blk), :]], axis=0)
        vb = jnp.concatenate([v4_ref[pl.ds(left, blk), :], v4_ref[pl.ds(r0, blk), :],
                              v4_ref[pl.ds(right, blk), :]], axis=0)
        ql = lax.broadcasted_iota(jnp.int32, (blk, blk), 0)
        kl = lax.broadcasted_iota(jnp.int32, (blk, blk), 1)
        left_ok = kl >= ql + jnp.where(n > 0, 0, blk)
        right_ok = kl + jnp.where(n < nb - 1, 0, blk) <= ql
        lo = lax.broadcasted_iota(jnp.int32, (blk, V7X_LANES), 1) < ATT_HEAD_DIM
        lo_band = lax.broadcasted_iota(jnp.int32, (3 * blk, V7X_LANES), 1) < ATT_HEAD_DIM
        ones_lo = jnp.where(lo_band, 1.0, 0.0).astype(BF16)
        ones_hi = jnp.where(lo_band, 0.0, 1.0).astype(BF16)
        outs = []
        for g in range(ATT_KV_HEADS):
            kv_lo = slice(2 * g * V7X_LANES, (2 * g + 1) * V7X_LANES)
            kv_hi = slice((2 * g + 1) * V7X_LANES, (2 * g + 2) * V7X_LANES)
            v_aug = jnp.concatenate([jnp.concatenate([vb[:, kv_lo], ones_lo], axis=1),
                                     jnp.concatenate([vb[:, kv_hi], ones_hi], axis=1)], axis=0)
            k_both = jnp.concatenate([kb[:, kv_lo], kb[:, kv_hi]], axis=0)
            for p in range(group // 2):
                pair = g * (group // 2) + p
                qp = qb[:, pair * V7X_LANES:(pair + 1) * V7X_LANES]
                s_both = _dot_nt(qp, k_both)
                probs, shifts = [], []
                for half in range(2):
                    sink = sink_ref[2 * pair + half] * LOG2E
                    s = s_both[:, half * 3 * blk:(half + 1) * 3 * blk]
                    s_l = jnp.where(left_ok, s[:, :blk], NEG_BIG)
                    s_m = s[:, blk:2 * blk]
                    s_r = jnp.where(right_ok, s[:, 2 * blk:], NEG_BIG)
                    m = jnp.max(jnp.maximum(jnp.maximum(s_l, s_r), s_m), axis=-1, keepdims=True)
                    m = jnp.maximum(m, sink)
                    probs += [jnp.exp2(s_l - m), jnp.exp2(s_m - m), jnp.exp2(s_r - m)]
                    shifts.append(sink - m)
                acc = _dot(jnp.concatenate(probs, axis=1).astype(BF16), v_aug)
                den = acc[:, V7X_LANES:] + jnp.exp2(jnp.where(lo, shifts[0], shifts[1]))
                outs.append(acc[:, :V7X_LANES] / den)
        o_ref[pl.ds(r0, blk), :] = jnp.concatenate(outs, axis=1).astype(BF16)
        return carry

    lax.fori_loop(0, nb, body, 0, unroll=16)


def _attention(q, k4, v4, sink, batch, seq):
    t = q.shape[0]
    by_batch = lambda w: pl.BlockSpec((seq, w), lambda b: (b, 0))
    return pl.pallas_call(
        functools.partial(_attn_kernel, seq=seq),
        out_shape=jax.ShapeDtypeStruct((t, ATT_WIDTH), BF16),
        grid=(batch,),
        in_specs=[pl.BlockSpec(memory_space=pltpu.SMEM), by_batch(ATT_WIDTH),
                  by_batch(4 * V7X_LANES), by_batch(4 * V7X_LANES)],
        out_specs=by_batch(ATT_WIDTH),
        compiler_params=_params(1),
        name="window_attn",
    )(sink, q, k4, v4)


def _scan(x, axis, reverse, op, identity):
    n = x.shape[axis]
    idx = lax.broadcasted_iota(jnp.int32, x.shape, axis)
    d = 1
    while d < n:
        if reverse:
            x = op(x, jnp.where(idx < n - d, pltpu.roll(x, n - d, axis), identity))
        else:
            x = op(x, jnp.where(idx >= d, pltpu.roll(x, d, axis), identity))
        d *= 2
    return x


def _mlstm_kernel(qraw_ref, kraw_ref, v_ref, og_ref, gr_ref, cw_ref, cb_ref, ng_ref,
                  y_ref,
                  ks_ref, qt_ref, kt_ref, vt_ref, rcol_ref, b_ref, dm_ref, w_ref, r_ref, am_ref, bt_ref,
                  cst_ref, mst_ref, cin_ref, minf_ref, minb_ref, hs_ref, *, seq):
    L = MLSTM_CHUNK
    H = MLSTM_HEADS
    dh = MLSTM_HEAD_DIM
    nc = seq // L
    width = MLSTM_WIDTH
    halo = 16
    row8 = lax.broadcasted_iota(jnp.int32, (8, L), 0)

    def head(h):
        return slice(h * dh, (h + 1) * dh)

    assert nc * 8 == V7X_LANES and L == V7X_LANES
    shape = (nc * 8, L)
    gi = gr_ref[:, 0:8, :].reshape(shape)
    lf = _log_sigmoid(gr_ref[:, 8:16, :].reshape(shape))
    fwd = (lax.broadcasted_iota(jnp.int32, shape, 0) & 7) < H
    lane = lax.broadcasted_iota(jnp.int32, shape, 1)
    b = jnp.where(fwd, _scan(lf, 1, False, jnp.add, 0.0), _scan(lf, 1, True, jnp.add, 0.0))
    b_tot = jnp.sum(jnp.where(lane == jnp.where(fwd, L - 1, 0), b, 0.0), axis=1, keepdims=True)
    a = b_tot - b + gi
    a_max = jnp.max(a, axis=1, keepdims=True)
    r = gi - b
    d_max = b + jnp.where(fwd, _scan(r, 1, False, jnp.maximum, NEG_BIG),
                          _scan(r, 1, True, jnp.maximum, NEG_BIG))
    r_ref[...] = (r * LOG2E).reshape(nc, 8, L)
    b_ref[...] = (b * LOG2E).reshape(nc, 8, L)
    dm_ref[...] = (d_max * LOG2E).reshape(nc, 8, L)
    w_ref[...] = jnp.exp(a - a_max).reshape(nc, 8, L)
    am_ref[...] = jnp.broadcast_to(a_max, shape).reshape(nc, 8, L)
    bt_ref[...] = jnp.broadcast_to(b_tot, shape).reshape(nc, 8, L)

    def prep(c, carry):
        r0 = pl.multiple_of(c * L, L)
        rows = pl.ds(r0, L)
        rid = lax.broadcasted_iota(jnp.int32, (L, width), 0)
        has_prev = jnp.where(c > 0, 1.0, 0.0)
        has_next = jnp.where(c < nc - 1, 1.0, 0.0)
        prev_at = pl.multiple_of(jnp.maximum(r0 - halo, 0), halo)
        next_at = pl.multiple_of(jnp.minimum(r0 + L, seq - halo), halo)
        acts = []
        for idx, raw_ref in enumerate((qraw_ref, kraw_ref)):
            cols = slice(idx * width, (idx + 1) * width)
            cur = raw_ref[rows, :].astype(F32)
            prev_row = raw_ref[pl.ds(prev_at, halo), :].astype(F32)[halo - 1:halo, :] * has_prev
            next_row = raw_ref[pl.ds(next_at, halo), :].astype(F32)[0:1, :] * has_next
            before = jnp.where(rid == 0, prev_row, pltpu.roll(cur, 1, 0))
            after = jnp.where(rid == L - 1, next_row, pltpu.roll(cur, L - 1, 0))
            u = (before * cw_ref[0:1, cols] + cur * cw_ref[1:2, cols] + after * cw_ref[2:3, cols]
                 + cb_ref[:, cols])
            u = u / (1.0 + jnp.exp2(u * (-LOG2E)))
            if idx == 1:
                u = u * (dh ** -0.5)
            acts.append(u)
        q_act, k_act = acts
        ks_ref[rows, :] = k_act.astype(BF16)
        ones_rows = jnp.ones((halo, L), BF16)
        for h in range(H):
            qt_ref[c, h] = q_act[:, head(h)].T.astype(BF16)
            kt_ref[c, h] = k_act[:, head(h)].T.astype(BF16)
            vt_ref[c, h, 0:dh, :] = v_ref[rows, head(h)].astype(F32).T.astype(BF16)
            vt_ref[c, h, dh:dh + halo, :] = ones_rows
        rcol_ref[rows, :] = jnp.concatenate([r_ref[c], jnp.zeros((L - 8, L), F32)], axis=0).T
        return carry

    lax.fori_loop(0, nc, prep, 0, unroll=4)

    cst_ref[...] = jnp.zeros_like(cst_ref)
    mst_ref[...] = jnp.zeros_like(mst_ref)

    def scan(i, carry):
        c_f = i
        c_b = nc - 1 - i
        fwd = row8 < H
        bt = jnp.where(fwd, bt_ref[c_f], bt_ref[c_b])
        am = jnp.where(fwd, am_ref[c_f], am_ref[c_b])
        m_old = mst_ref[...]
        m_new = jnp.maximum(bt + m_old, am)
        s_prev = jnp.exp(bt + m_old - m_new)
        s_loc = jnp.exp(am - m_new)
        minf_ref[c_f] = m_old
        minb_ref[c_b] = m_old
        mst_ref[...] = m_new
        for d, c in enumerate((c_f, c_b)):
            wr = w_ref[c]
            for h in range(H):
                j = d * H + h
                kw_t = (kt_ref[c, h].astype(F32) * wr[j:j + 1, :]).astype(BF16)
                c_loc = _dot_nt(vt_ref[c, h], kw_t)
                c_old = cst_ref[j]
                cin_ref[c, j] = c_old.astype(BF16)
                cst_ref[j] = s_prev[j:j + 1, :] * c_old + s_loc[j:j + 1, :] * c_loc
        return carry

    lax.fori_loop(0, nc, scan, 0, unroll=8)

    def emit(c):
        rows = pl.ds(pl.multiple_of(c * L, L), L)
        m_in = jnp.where(row8 < H, minf_ref[c], minb_ref[c]) * LOG2E
        bc = b_ref[c]
        m_t = jnp.maximum(bc + m_in, dm_ref[c])
        b_rel = bc - m_t
        scale_in = jnp.exp2(b_rel + m_in)
        floor = jnp.exp2(-m_t)
        rcol = rcol_ref[rows, :]
        s_id = lax.broadcasted_iota(jnp.int32, (L, L), 0)
        t_id = lax.broadcasted_iota(jnp.int32, (L, L), 1)
        masks = (s_id <= t_id, s_id >= t_id)
        for h in range(H):
            q_t = qt_ref[c, h]
            s_t = _dot(ks_ref[rows, head(h)], q_t)
            weights, sums, inters = [], [], []
            for d in range(2):
                j = d * H + h
                logw = jnp.broadcast_to(rcol[:, j:j + 1], (L, L)) + b_rel[j:j + 1, :]
                sc = s_t * jnp.exp2(jnp.where(masks[d], logw, NEG_BIG))
                sums.append(jnp.sum(sc, axis=0, keepdims=True))
                weights.append(sc.astype(BF16))
                inters.append(_dot(cin_ref[c, j], q_t))
            num = _dot(vt_ref[c, h, 0:dh, :], jnp.concatenate(weights, axis=1))
            hsum = None
            for d in range(2):
                j = d * H + h
                den = sums[d] + scale_in[j:j + 1, :] * inters[d][dh:dh + 1, :]
                inv = 1.0 / jnp.maximum(jnp.abs(den), floor[j:j + 1, :])
                hd = num[:, d * L:(d + 1) * L] * inv + inters[d][0:dh, :] * (scale_in[j:j + 1, :] * inv)
                hsum = hd if hsum is None else hsum + hd
            hs_ref[c, h] = hsum

    def finish(c):
        rows = pl.ds(pl.multiple_of(c * L, L), L)
        for h in range(H):
            hsum = hs_ref[c, h]
            xc = hsum - jnp.mean(hsum, axis=0, keepdims=True)
            var = jnp.mean(xc * xc, axis=0, keepdims=True)
            y = (xc * lax.rsqrt(var + NORM_EPS)).T * ng_ref[:, head(h)] * og_ref[rows, head(h)].astype(F32)
            y_ref[rows, head(h)] = y.astype(BF16)

    def emit_and_finish(c, carry):
        finish(c - 1)
        emit(c)
        return carry

    emit(0)
    lax.fori_loop(1, nc, emit_and_finish, 0, unroll=5)
    finish(nc - 1)


def _mlstm(qraw, kraw, v, og, grow, conv_w, conv_b, norm_g, batch, seq):
    t = qraw.shape[0]
    nc = seq // MLSTM_CHUNK
    nd = 2 * MLSTM_HEADS
    by_batch = lambda w: pl.BlockSpec((seq, w), lambda b: (b, 0))
    dh = MLSTM_HEAD_DIM
    aug = dh + 16
    tile = (nc, MLSTM_HEADS, dh, MLSTM_CHUNK)
    rowvec = pltpu.VMEM((nc, 8, MLSTM_CHUNK), F32)
    scratch = [
        pltpu.VMEM((seq, MLSTM_WIDTH), BF16),
        pltpu.VMEM(tile, BF16),
        pltpu.VMEM(tile, BF16),
        pltpu.VMEM((nc, MLSTM_HEADS, aug, MLSTM_CHUNK), BF16),
        pltpu.VMEM((seq, V7X_LANES), F32),
        rowvec,
        rowvec,
        rowvec,
        rowvec,
        rowvec,
        rowvec,
        pltpu.VMEM((nd, aug, dh), F32),
        pltpu.VMEM((8, MLSTM_CHUNK), F32),
        pltpu.VMEM((nc, nd, aug, dh), BF16),
        rowvec,
        rowvec,
        pltpu.VMEM(tile, F32),
    ]
    return pl.pallas_call(
        functools.partial(_mlstm_kernel, seq=seq),
        out_shape=jax.ShapeDtypeStruct((t, MLSTM_WIDTH), BF16),
        grid=(batch,),
        in_specs=[by_batch(MLSTM_WIDTH)] * 4 + [
            pl.BlockSpec((nc, 16, MLSTM_CHUNK), lambda b: (b, 0, 0)),
            _resident(conv_w.shape), _resident(conv_b.shape), _resident(norm_g.shape)],
        out_specs=by_batch(MLSTM_WIDTH),
        scratch_shapes=scratch,
        compiler_params=_params(1),
        name="bidir_mlstm",
    )(qraw, kraw, v, og, grow, conv_w, conv_b, norm_g)


def _merge_kernel(x_ref, ya_ref, ym_ref, ga_ref, gm_ref, wa_ref, wb_ref, wo_ref, o_ref):
    merged = (ga_ref[...].astype(F32) * _dot(ya_ref[...], wa_ref[...])
              + gm_ref[...].astype(F32) * _dot(ym_ref[...], wb_ref[...]))
    o_ref[...] = x_ref[...] + _dot(merged.astype(BF16), wo_ref[...])


def _merge(x, ya, ym, ga, gm, wa, wb, wo, layer):
    t, d = x.shape
    tm = MERGE_ROWS
    return pl.pallas_call(
        _merge_kernel,
        out_shape=jax.ShapeDtypeStruct((t, d), F32),
        grid=(t // tm,),
        in_specs=[_rows(tm, d), _rows(tm, ATT_WIDTH), _rows(tm, MLSTM_WIDTH), _rows(tm, d), _rows(tm, d),
                  _layer(wa.shape, layer), _layer(wb.shape, layer), _layer(wo.shape, layer)],
        out_specs=_rows(tm, d),
        compiler_params=_params(1),
        name="merge_out",
    )(x, ya, ym, ga, gm, wa, wb, wo)


def _w_in_offsets():
    o_v = ATT_WIDTH + 2 * ATT_KV_WIDTH
    o_km = o_v + 2 * MLSTM_WIDTH
    o_om = o_km + 2 * MLSTM_WIDTH
    o_g = o_om + 4 * MLSTM_HEADS
    return o_v, o_km, o_om, o_g


def _split_kernel(wt_ref, qkv_ref, qkm_ref, vo_ref, gate_ref, mg_ref):
    o_v, o_km, o_om, o_g = _w_in_offsets()
    qkv_ref[0] = wt_ref[0, :o_v, :].T.astype(BF16)
    qkm_ref[0] = wt_ref[0, o_v:o_km, :].T.astype(BF16)
    vo_ref[0] = wt_ref[0, o_km:o_om, :].T.astype(BF16)
    gate_ref[0] = wt_ref[0, o_om:o_g, :]
    mg_ref[0] = wt_ref[0, o_g:, :].T.astype(BF16)


def _split_w_in(w_in):
    depth, rows, cols = w_in.shape
    o_v, o_km, o_om, o_g = _w_in_offsets()
    widths = (o_v, o_km - o_v, o_om - o_km, cols - o_g)
    out_shape = [jax.ShapeDtypeStruct((depth, rows, w), BF16) for w in widths]
    out_specs = [pl.BlockSpec((1, CAST_ROWS, w), lambda l, i: (l, i, 0)) for w in widths]
    out_shape.insert(3, jax.ShapeDtypeStruct((depth, o_g - o_om, rows), F32))
    out_specs.insert(3, pl.BlockSpec((1, o_g - o_om, CAST_ROWS), lambda l, i: (l, 0, i)))
    return pl.pallas_call(
        _split_kernel,
        out_shape=out_shape,
        grid=(depth, rows // CAST_ROWS),
        in_specs=[pl.BlockSpec((1, cols, CAST_ROWS), lambda l, i: (l, 0, i))],
        out_specs=out_specs,
        compiler_params=_params(2),
        name="split_w_in",
    )(jnp.swapaxes(w_in, 1, 2))


def _proj_weights(wg_t, gate_bias, q_gain, k_gain):
    H = MLSTM_HEADS
    gate_row = jnp.concatenate([wg_t[0:H], wg_t[2 * H:3 * H], wg_t[H:2 * H], wg_t[3 * H:4 * H]], axis=0)
    b_i = jnp.concatenate([gate_bias[0:H], gate_bias[2 * H:3 * H]])
    b_f = jnp.concatenate([gate_bias[H:2 * H], gate_bias[3 * H:4 * H]])
    return {
        "gate_row": gate_row.astype(BF16),
        "bias_row": jnp.concatenate([b_i, b_f]).reshape(4 * H, 1).astype(F32),
        "q_gain": jnp.tile(q_gain, ATT_HEADS).reshape(1, ATT_WIDTH).astype(F32),
        "k_gain": jnp.tile(k_gain, ATT_KV_HEADS).reshape(1, ATT_KV_WIDTH).astype(F32),
    }


def kernel(x, positions, ffn1_norm, ffn1_w_gate, ffn1_w_up, ffn1_w_down, mix_norm, w_in, mlstm_gate_bias, attn_q_norm, attn_k_norm, attn_sink, mlstm_conv_w, mlstm_conv_b, mlstm_out_norm, w_branch_attn, w_branch_mlstm, w_out, ffn2_norm, ffn2_w_gate, ffn2_w_up, ffn2_w_down, block_out_norm):
    batch, seq, d = x.shape
    depth = w_in.shape[0]
    t = batch * seq
    xt = x.reshape(t, d)
    cos_t, sin_t = _rope_tables(positions)
    wg1, wu1, wg2, wu2 = _to_bf16(ffn1_w_gate, ffn1_w_up, ffn2_w_gate, ffn2_w_up)
    wd1, wd2 = _to_bf16(ffn1_w_down, ffn2_w_down)
    wa, wb = _to_bf16(w_branch_attn, w_branch_mlstm)
    (wo,) = _to_bf16(w_out)
    w_qkv, w_qkm, w_vo, w_gate, w_mg = _split_w_in(w_in)
    big = {"qkv": w_qkv, "qkm": w_qkm, "vo": w_vo, "merge": w_mg}
    for l in range(depth):
        xt = _ffn(xt, ffn1_norm[l], wg1, wu1, wd1, l)
        pw = _proj_weights(w_gate[l], mlstm_gate_bias[l], attn_q_norm[l], attn_k_norm[l])
        qa, k4, v4, qm, km, vm, og, grow, ga, gm = _proj(xt, mix_norm[l], big, l, pw, cos_t, sin_t)
        ya = _attention(qa, k4, v4, attn_sink[l].astype(F32), batch, seq)
        ym = _mlstm(qm, km, vm, og, grow, mlstm_conv_w[l].astype(F32),
                    mlstm_conv_b[l].reshape(1, -1).astype(F32),
                    mlstm_out_norm[l].reshape(1, -1).astype(F32), batch, seq)
        xt = _merge(xt, ya, ym, ga, gm, wa, wb, wo, l)
        xt = _ffn(xt, ffn2_norm[l], wg2, wu2, wd2, l, final_gain=block_out_norm[l])
    return xt.reshape(batch, seq, d)
```
